```python
import jax, jax.numpy as jnp
from jax import lax
import numpy as np

D_MODEL = 1024
BATCH = 2
SEQ = 8192
DEPTH = 4
DEC_BATCH = 128
DEC_SEQ = 8
PAST_LEN = 8192
PAGE_SIZE = 128

N_PAGES = PAST_LEN // PAGE_SIZE

HEAD_DIM = 64
ROT_DIM = HEAD_DIM // 4
ROPE_THETA = 500000.0
NORM_EPS = 1e-6
Q_BLOCK = 128
HEAD_SCALE = HEAD_DIM ** -0.5

MLA_HEADS = 8
MLA_NOPE = 64
MLA_ROPE = 32
MLA_V = 64
Q_LORA = 384
KV_LORA = 256
LAT_W = KV_LORA + MLA_ROPE
MLA_SCALE = (MLA_NOPE + MLA_ROPE) ** -0.5

MOBA_HEADS = 8
MOBA_KV_HEADS = 4
MOBA_BLOCK = 256
MOBA_TOPK = 3

AB_IN = Q_LORA + KV_LORA + MLA_ROPE + (MOBA_HEADS + 2 * MOBA_KV_HEADS) * HEAD_DIM
AB_OUT = MLA_HEADS * MLA_V + MOBA_HEADS * HEAD_DIM

DIL_CFG = ((128, 1), (512, 4), (2048, 16))
N_DIL = len(DIL_CFG)
DIL_HEADS = 8

D_FF = 2816
CONV_W = 3

N_EVEN = (DEPTH + 1) // 2
N_ODD = DEPTH // 2

kernel_name = 'hybrid_mla_moba_dilated_convffn_step'


def rmsnorm(x, g):
    xf = x.astype(jnp.float32)
    y = xf * lax.rsqrt(jnp.mean(xf * xf, axis=-1, keepdims=True) + NORM_EPS)
    return (y * g.astype(jnp.float32)).astype(x.dtype)


def rope(x, pos, rot):
    half = rot // 2
    inv = ROPE_THETA ** (-jnp.arange(half, dtype=jnp.float32) / half)
    ang = pos.astype(jnp.float32)[:, None] * inv[None, :]
    cos = jnp.cos(ang)[None, :, None, :].astype(x.dtype)
    sin = jnp.sin(ang)[None, :, None, :].astype(x.dtype)
    x1, x2, rest = x[..., :half], x[..., half:rot], x[..., rot:]
    return jnp.concatenate([x1 * cos - x2 * sin, x2 * cos + x1 * sin, rest], axis=-1)


def pad_rows(a, mult):
    n = (-a.shape[1]) % mult
    return jnp.pad(a, [(0, 0), (0, n)] + [(0, 0)] * (a.ndim - 2))


def gather_pages(pool, page_table):
    pages = pool[page_table]
    return pages.reshape(page_table.shape[0], page_table.shape[1] * pool.shape[1], *pool.shape[2:])


def sweep_queries(fn, n_q, *qs):
    nb = n_q // Q_BLOCK
    xs = tuple(jnp.swapaxes(a.reshape(a.shape[0], nb, Q_BLOCK, *a.shape[2:]), 0, 1) for a in qs)

    def step(args):
        return fn(args[0] * Q_BLOCK + jnp.arange(Q_BLOCK), *args[1:])

    out = jnp.swapaxes(lax.map(step, (jnp.arange(nb), *xs)), 0, 1)
    return out.reshape(out.shape[0], n_q, *out.shape[3:])


def ab_project(h, pos, w_in, g_q, g_kv, w_uq, w_uk):
    B, T = h.shape[:2]
    z = h @ w_in
    o1 = Q_LORA
    o2 = o1 + KV_LORA
    o3 = o2 + MLA_ROPE
    o4 = o3 + MOBA_HEADS * HEAD_DIM
    o5 = o4 + MOBA_KV_HEADS * HEAD_DIM
    cq = rmsnorm(z[..., :o1], g_q)
    ckv = rmsnorm(z[..., o1:o2], g_kv)
    kr = rope(z[..., o2:o3][:, :, None, :], pos, MLA_ROPE)[:, :, 0]
    q = jnp.einsum('btc,chd->bthd', cq, w_uq)
    q_lat = jnp.einsum('bthn,hcn->bthc', q[..., :MLA_NOPE], w_uk)
    q_rope = rope(q[..., MLA_NOPE:], pos, MLA_ROPE)
    q_cat = jnp.concatenate([q_lat, q_rope], axis=-1)
    lat = jnp.concatenate([ckv, kr], axis=-1)
    qb = rope(z[..., o3:o4].reshape(B, T, MOBA_HEADS, HEAD_DIM), pos, ROT_DIM)
    kb = rope(z[..., o4:o5].reshape(B, T, MOBA_KV_HEADS, HEAD_DIM), pos, ROT_DIM)
    vb = z[..., o5:].reshape(B, T, MOBA_KV_HEADS, HEAD_DIM)
    return q_cat, lat, qb, jnp.stack([kb, vb], axis=2)


def mla_attend(q_cat, q_pos, lat, w_uv):
    s = jnp.einsum('bthc,blc->bhtl', q_cat, lat).astype(jnp.float32) * MLA_SCALE
    mask = jnp.arange(lat.shape[1])[None, :] <= q_pos[:, None]
    p = jax.nn.softmax(jnp.where(mask, s, -jnp.inf), axis=-1).astype(lat.dtype)
    o_lat = jnp.einsum('bhtl,blc->bthc', p, lat[..., :KV_LORA])
    return jnp.einsum('bthc,hcv->bthv', o_lat, w_uv)


def block_means(kv):
    B, Lp = kv.shape[:2]
    k = kv[:, :, 0].astype(jnp.float32)
    return k.reshape(B, Lp // MOBA_BLOCK, MOBA_BLOCK, MOBA_KV_HEADS, HEAD_DIM).mean(axis=2)


def moba_attend(q, q_pos, kv, k_means):
    B, T = q.shape[:2]
    nb = k_means.shape[1]
    n_sel = min(MOBA_TOPK, nb)
    grp = MOBA_HEADS // MOBA_KV_HEADS
    qg = q.reshape(B, T, MOBA_KV_HEADS, grp, HEAD_DIM)
    cur = q_pos // MOBA_BLOCK
    gate = jnp.einsum('btkgd,bnkd->bkgtn', qg.astype(jnp.float32), k_means)
    past = jnp.arange(nb)[None, :] < cur[:, None]
    top_val, sel = lax.top_k(jnp.where(past, gate, -jnp.inf), n_sel)
    sel_ok = jnp.isfinite(top_val)
    kv_blk = kv.reshape(B, nb, MOBA_BLOCK, 2, MOBA_KV_HEADS, HEAD_DIM)
    b_idx = jnp.arange(B)[:, None, None, None, None]
    h_idx = jnp.arange(MOBA_KV_HEADS)[None, :, None, None, None]
    k_sel = kv_blk[b_idx, sel, :, 0, h_idx]
    v_sel = kv_blk[b_idx, sel, :, 1, h_idx]
    own = (cur * MOBA_BLOCK)[:, None] + jnp.arange(MOBA_BLOCK)[None, :]
    own_ok = own <= q_pos[:, None]
    k_own = kv[:, own, 0]
    v_own = kv[:, own, 1]
    qt = jnp.transpose(qg, (0, 2, 3, 1, 4))
    s_sel = jnp.einsum('bkgtd,bkgtnjd->bkgtnj', qt, k_sel).astype(jnp.float32)
    s_sel = jnp.where(sel_ok[..., None], s_sel, -jnp.inf).reshape(B, MOBA_KV_HEADS, grp, T, n_sel * MOBA_BLOCK)
    s_own = jnp.einsum('bkgtd,btjkd->bkgtj', qt, k_own).astype(jnp.float32)
    s_own = jnp.where(own_ok, s_own, -jnp.inf)
    p = jax.nn.softmax(jnp.concatenate([s_sel, s_own], axis=-1) * HEAD_SCALE, axis=-1).astype(q.dtype)
    p_sel = p[..., :n_sel * MOBA_BLOCK].reshape(B, MOBA_KV_HEADS, grp, T, n_sel, MOBA_BLOCK)
    p_own = p[..., n_sel * MOBA_BLOCK:]
    o = (jnp.einsum('bkgtnj,bkgtnjd->bkgtd', p_sel, v_sel)
         + jnp.einsum('bkgtj,btjkd->bkgtd', p_own, v_own))
    return jnp.transpose(o, (0, 3, 1, 2, 4)).reshape(B, T, MOBA_HEADS, HEAD_DIM)


def ab_merge(oa, ob, w_o):
    B, T = oa.shape[:2]
    return jnp.concatenate([oa.reshape(B, T, -1), ob.reshape(B, T, -1)], axis=-1) @ w_o


def c_project(h, pos, w_qkv):
    B, T = h.shape[:2]
    nh = N_DIL * DIL_HEADS
    z = (h @ w_qkv).reshape(B, T, 3, nh, HEAD_DIM)
    shp = (B, T, N_DIL, DIL_HEADS, HEAD_DIM)
    q = rope(z[:, :, 0], pos, ROT_DIM).reshape(shp)
    k = rope(z[:, :, 1], pos, ROT_DIM).reshape(shp)
    v = z[:, :, 2].reshape(shp)
    return q, jnp.stack([k, v], axis=3)


def dilated_mix(q, q_pos, kv_exts, pos0s):
    outs, lses = [], []
    for g, (win, dil) in enumerate(DIL_CFG):
        offs = dil * jnp.arange(win // dil + 1)
        kpos = q_pos[:, None] - offs[None, :]
        idx = kpos - pos0s[g]
        k = kv_exts[g][:, idx, 0]
        v = kv_exts[g][:, idx, 1]
        s = jnp.einsum('bthd,btohd->btho', q[:, :, g], k).astype(jnp.float32) * HEAD_SCALE
        s = jnp.where((kpos >= 0)[None, :, None, :], s, -jnp.inf)
        lse = jax.nn.logsumexp(s, axis=-1)
        p = jnp.exp(s - lse[..., None]).astype(q.dtype)
        outs.append(jnp.einsum('btho,btohd->bthd', p, v))
        lses.append(lse)
    alpha = jax.nn.softmax(jnp.stack(lses), axis=0)
    o = jnp.einsum('gbth,gbthd->bthd', alpha, jnp.stack(outs).astype(jnp.float32))
    return o.astype(q.dtype)


def conv_ffn(x, hist, g, w_up, w_conv, b_conv, w_down):
    T = x.shape[1]
    gu = rmsnorm(x, g) @ w_up
    gate, up = gu[..., :D_FF], gu[..., D_FF:]
    ext = jnp.concatenate([hist, gate], axis=1)
    conv = b_conv
    for i in range(CONV_W):
        conv = conv + w_conv[i] * ext[:, i:i + T]
    y = (jax.nn.silu(conv) * up) @ w_down
    return x + y, ext[:, ext.shape[1] - (CONV_W - 1):]


def setup_inputs(seed: int = 0) -> dict:
    key = jax.random.key(seed)
    ks = iter(jax.random.split(key, 32))
    f32 = jnp.float32

    def nrm(shape, scale=1.0):
        return jax.random.normal(next(ks), shape, f32) * scale

    def gain(shape):
        return 1.0 + nrm(shape, 0.05)

    n_used = DEC_BATCH * N_PAGES
    n_pool = n_used + max(1, n_used // 4)
    win_rows = sum(min(w, PAST_LEN) for w, _ in DIL_CFG)
    c_width = N_DIL * DIL_HEADS * HEAD_DIM
    d = {}
    d['x_prompt'] = nrm((BATCH, SEQ, D_MODEL))
    d['x_sample'] = nrm((DEC_BATCH, DEC_SEQ, D_MODEL))
    d['cache_mla_l0'] = nrm((n_pool, PAGE_SIZE, LAT_W))
    d['cache_moba_l0'] = nrm((n_pool, PAGE_SIZE, 2, MOBA_KV_HEADS, HEAD_DIM))
    d['state_win_l1'] = nrm((DEC_BATCH, win_rows, 2, DIL_HEADS, HEAD_DIM))
    d['cache_mla_l2'] = nrm((n_pool, PAGE_SIZE, LAT_W))
    d['cache_moba_l2'] = nrm((n_pool, PAGE_SIZE, 2, MOBA_KV_HEADS, HEAD_DIM))
    d['state_win_l3'] = nrm((DEC_BATCH, win_rows, 2, DIL_HEADS, HEAD_DIM))
    d['state_ffn_conv'] = nrm((DEPTH, DEC_BATCH, CONV_W - 1, D_FF))
    d['page_table'] = jax.random.permutation(next(ks), n_pool)[:n_used].reshape(DEC_BATCH, N_PAGES).astype(jnp.int32)
    d['g_mix'] = gain((DEPTH, D_MODEL))
    d['g_ffn'] = gain((DEPTH, D_MODEL))
    d['g_final'] = gain((D_MODEL,))
    d['w_in_ab'] = nrm((N_EVEN, D_MODEL, AB_IN), D_MODEL ** -0.5)
    d['g_q_lat'] = gain((N_EVEN, Q_LORA))
    d['g_kv_lat'] = gain((N_EVEN, KV_LORA))
    d['w_uq'] = nrm((N_EVEN, Q_LORA, MLA_HEADS, MLA_NOPE + MLA_ROPE), Q_LORA ** -0.5)
    d['w_uk'] = nrm((N_EVEN, MLA_HEADS, KV_LORA, MLA_NOPE), KV_LORA ** -0.5)
    d['w_uv'] = nrm((N_EVEN, MLA_HEADS, KV_LORA, MLA_V), KV_LORA ** -0.5)
    d['w_o_ab'] = nrm((N_EVEN, AB_OUT, D_MODEL), AB_OUT ** -0.5)
    d['w_qkv_c'] = nrm((N_ODD, D_MODEL, 3 * c_width), D_MODEL ** -0.5)
    d['w_o_c'] = nrm((N_ODD, DIL_HEADS * HEAD_DIM, D_MODEL), (DIL_HEADS * HEAD_DIM) ** -0.5)
    d['w_up'] = nrm((DEPTH, D_MODEL, 2 * D_FF), D_MODEL ** -0.5)
    d['w_conv'] = nrm((DEPTH, CONV_W, D_FF), CONV_W ** -0.5)
    d['b_conv'] = nrm((DEPTH, D_FF), 0.02)
    d['w_down'] = nrm((DEPTH, D_FF, D_MODEL), D_FF ** -0.5)
    return d


def reference(x_prompt, x_sample, cache_mla_l0, cache_moba_l0, state_win_l1, cache_mla_l2, cache_moba_l2,
              state_win_l3, state_ffn_conv, page_table, g_mix, g_ffn, g_final, w_in_ab, g_q_lat, g_kv_lat,
              w_uq, w_uk, w_uv, w_o_ab, w_qkv_c, w_o_c, w_up, w_conv, b_conv, w_down):
    n_p, s_p = x_prompt.shape[:2]
    n_s, s_s = x_sample.shape[:2]
    pos_p = jnp.arange(s_p)
    pos_s = PAST_LEN + jnp.arange(s_s)
    mla_caches = (cache_mla_l0, cache_mla_l2)
    moba_caches = (cache_moba_l0, cache_moba_l2)
    win_states = (state_win_l1, state_win_l3)
    lat_p, lat_s, kvb_p, kvb_s, win_p, win_s, conv_p, conv_s = [], [], [], [], [], [], [], []
    xp, xs = x_prompt, x_sample
    for layer in range(DEPTH):
        i = layer // 2
        hp = rmsnorm(xp, g_mix[layer])
        hs = rmsnorm(xs, g_mix[layer])
        if layer % 2 == 0:
            qc, lat, qb, kv = ab_project(hp, pos_p, w_in_ab[i], g_q_lat[i], g_kv_lat[i], w_uq[i], w_uk[i])
            oa = sweep_queries(lambda qp, qcb: mla_attend(qcb, qp, lat, w_uv[i]), s_p, qc)
            kv_pad = pad_rows(kv, MOBA_BLOCK)
            means = block_means(kv_pad)
            ob = sweep_queries(lambda qp, qbb: moba_attend(qbb, qp, kv_pad, means), s_p, qb)
            xp = xp + ab_merge(oa, ob, w_o_ab[i])
            lat_p.append(lat)
            kvb_p.append(kv)
            qc2, lat2, qb2, kv2 = ab_project(hs, pos_s, w_in_ab[i], g_q_lat[i], g_kv_lat[i], w_uq[i], w_uk[i])
            lat_all = jnp.concatenate([gather_pages(mla_caches[i], page_table), lat2], axis=1)
            oa2 = mla_attend(qc2, pos_s, lat_all, w_uv[i])
            kv_all = pad_rows(jnp.concatenate([gather_pages(moba_caches[i], page_table), kv2], axis=1), MOBA_BLOCK)
            ob2 = moba_attend(qb2, pos_s, kv_all, block_means(kv_all))
            xs = xs + ab_merge(oa2, ob2, w_o_ab[i])
            lat_s.append(lat2)
            kvb_s.append(kv2)
        else:
            q, kvc = c_project(hp, pos_p, w_qkv_c[i])
            exts = [jnp.pad(kvc[:, :, g], ((0, 0), (win, 0), (0, 0), (0, 0), (0, 0))) for g, (win, _) in enumerate(DIL_CFG)]
            pos0 = [-win for win, _ in DIL_CFG]
            oc = sweep_queries(lambda qp, qcb: dilated_mix(qcb, qp, exts, pos0), s_p, q)
            xp = xp + oc.reshape(n_p, s_p, -1) @ w_o_c[i]
            win_p.append(jnp.concatenate([kvc[:, s_p - min(win, s_p):, g] for g, (win, _) in enumerate(DIL_CFG)], axis=1))
            q2, kvc2 = c_project(hs, pos_s, w_qkv_c[i])
            exts2, bufs2, start = [], [], 0
            for g, (win, _) in enumerate(DIL_CFG):
                wb = min(win, PAST_LEN)
                hist = jnp.concatenate([win_states[i][:, start:start + wb], kvc2[:, :, g]], axis=1)
                start += wb
                exts2.append(jnp.pad(hist, ((0, 0), (win - wb, 0), (0, 0), (0, 0), (0, 0))))
                bufs2.append(hist[:, hist.shape[1] - min(win, PAST_LEN + s_s):])
            oc2 = dilated_mix(q2, pos_s, exts2, [PAST_LEN - win for win, _ in DIL_CFG])
            xs = xs + oc2.reshape(n_s, s_s, -1) @ w_o_c[i]
            win_s.append(jnp.concatenate(bufs2, axis=1))
        xp, cp = conv_ffn(xp, jnp.zeros((n_p, CONV_W - 1, D_FF), xp.dtype), g_ffn[layer],
                          w_up[layer], w_conv[layer], b_conv[layer], w_down[layer])
        xs, cs = conv_ffn(xs, state_ffn_conv[layer], g_ffn[layer],
                          w_up[layer], w_conv[layer], b_conv[layer], w_down[layer])
        conv_p.append(cp)
        conv_s.append(cs)
    y_prompt = rmsnorm(xp, g_final)
    y_sample = rmsnorm(xs, g_final)
    ffn_conv_prompt = jnp.stack(conv_p)
    ffn_conv_sample = jnp.stack(conv_s)
    return (y_prompt, y_sample,
            lat_p[0], lat_s[0], kvb_p[0], kvb_s[0], win_p[0], win_s[0],
            lat_p[1], lat_s[1], kvb_p[1], kvb_s[1], win_p[1], win_s[1],
            ffn_conv_prompt, ffn_conv_sample)
```

```python
import functools

import jax
import jax.numpy as jnp
from jax import lax
from jax.experimental import pallas as pl
from jax.experimental.pallas import tpu as pltpu

F32 = jnp.float32
BF16 = jnp.bfloat16

D_MODEL = 1024
PAGE_SIZE = 128
HEAD_DIM = 64
ROT_DIM = HEAD_DIM // 4
ROPE_THETA = 500000.0
NORM_EPS = 1e-6
HEAD_SCALE = HEAD_DIM ** -0.5

MLA_HEADS = 8
MLA_NOPE = 64
MLA_ROPE = 32
MLA_V = 64
Q_LORA = 384
KV_LORA = 256
LAT_W = KV_LORA + MLA_ROPE
MLA_SCALE = (MLA_NOPE + MLA_ROPE) ** -0.5

MOBA_HEADS = 8
MOBA_KV_HEADS = 4
MOBA_BLOCK = 256
MOBA_TOPK = 3
MOBA_Q_W = MOBA_HEADS * HEAD_DIM
MOBA_KV_W = MOBA_KV_HEADS * HEAD_DIM

DIL_CFG = ((128, 1), (512, 4), (2048, 16))
N_DIL = len(DIL_CFG)
DIL_HEADS = 8
DIL_W = DIL_HEADS * HEAD_DIM
DIL_KEYS = 128
WIN_ROWS = sum(w for w, _ in DIL_CFG)

D_FF = 2816
CONV_W = 3

VMEM_LIMIT = 56 * 1024 * 1024

ROW_TILE = 256
MLA_TQ = 128
MLA_TK = 512
DIL_TQ = 128
PAGES_PER_STEP = 16
WIN_CHUNK = 896

_NT = (((1,), (1,)), ((), ()))


def _cparams(*sem):
    return pltpu.CompilerParams(dimension_semantics=sem, vmem_limit_bytes=VMEM_LIMIT)


def _const_spec(shape):
    nd = len(shape)
    return pl.BlockSpec(shape, lambda *_: (0,) * nd, pipeline_mode=pl.Buffered(1))


def _rms(x, g):
    return x * lax.rsqrt(jnp.mean(x * x, axis=-1, keepdims=True) + NORM_EPS) * g


def _rope64(x, tab):
    w = x.shape[1]
    n = w // 128
    c = jnp.tile(tab[:, 0:128], (1, n))
    s1 = jnp.tile(tab[:, 128:256], (1, n))
    s2 = jnp.tile(tab[:, 256:384], (1, n))
    half = ROT_DIM // 2
    x_hi = pltpu.roll(x, w - half, 1)
    x_lo = pltpu.roll(x, half, 1)
    return x * c + x_hi * s1 + x_lo * s2


def _bdot(a, b):
    return jnp.dot(a, b, preferred_element_type=F32)


def _online_update(s, v, m, l, acc):
    m_new = jnp.maximum(m, jnp.max(s, axis=1, keepdims=True))
    alpha = jnp.exp(m - m_new)
    p = jnp.exp(s - m_new)
    l_new = alpha * l + jnp.sum(p, axis=1, keepdims=True)
    acc_new = alpha * acc + _bdot(p.astype(BF16), v)
    return m_new, l_new, acc_new


def _ab_kernel(with_means, x_ref, gm_ref, win_ref, gq_ref, gkv_ref, wq2_ref, wukt_ref, t64_ref, t32_ref,
               q_ref, lat_ref, latb_ref, qb_ref, kv_ref, kvb_ref, *mean_ref):
    h = _rms(x_ref[...], gm_ref[...]).astype(BF16)
    z = _bdot(h, win_ref[...])
    t64 = t64_ref[...]
    t32 = t32_ref[...]
    qk = _rope64(z[:, :MOBA_Q_W + MOBA_KV_W], t64)
    qb_ref[...] = qk[:, :MOBA_Q_W]
    kv = jnp.concatenate([qk[:, MOBA_Q_W:], z[:, MOBA_Q_W + MOBA_KV_W:1024]], axis=-1)
    kv_ref[...] = kv
    kvb_ref[...] = kv.astype(BF16)
    if with_means:
        mean_ref[0][0] = jnp.sum(qk[:, MOBA_Q_W:], axis=0, keepdims=True) * (1.0 / MOBA_BLOCK)
    zl = z[:, 1024 + Q_LORA:]
    ckv = _rms(zl[:, :KV_LORA], gkv_ref[...])
    kr = (zl[:, KV_LORA:KV_LORA + MLA_ROPE] * t32[:, 0:MLA_ROPE]
          + zl[:, KV_LORA + MLA_ROPE:] * t32[:, 128:128 + MLA_ROPE])
    lat = jnp.concatenate([ckv, kr], axis=-1)
    lat_ref[...] = lat
    latb_ref[...] = lat.astype(BF16)
    cq = _rms(z[:, 1024:1024 + Q_LORA], gq_ref[...]).astype(BF16)
    q2 = _bdot(cq, wq2_ref[...])
    c256 = jnp.tile(t32[:, 0:128], (1, 2))
    s256 = jnp.tile(t32[:, 128:256], (1, 2))
    qr = (q2[:, 512:768] * c256 + q2[:, 768:1024] * s256) * MLA_SCALE
    for hh in range(MLA_HEADS):
        ql = _bdot(q2[:, MLA_NOPE * hh:MLA_NOPE * (hh + 1)].astype(BF16), wukt_ref[hh]) * MLA_SCALE
        q_ref[hh, :, 0:KV_LORA] = ql.astype(q_ref.dtype)
        q_ref[hh, :, KV_LORA:LAT_W] = qr[:, MLA_ROPE * hh:MLA_ROPE * (hh + 1)].astype(q_ref.dtype)


def _ab_project(x, g_mix, wts, t64, t32, q_dtype, with_means):
    rows = x.shape[0]
    tm = ROW_TILE
    n_tab = t64.shape[0] // tm
    row = lambda w: pl.BlockSpec((tm, w), lambda i: (i, 0))
    tab = lambda w: pl.BlockSpec((tm, w), lambda i: (i % n_tab, 0))
    out_shape = [jax.ShapeDtypeStruct((MLA_HEADS, rows, LAT_W), q_dtype),
                 jax.ShapeDtypeStruct((rows, LAT_W), F32),
                 jax.ShapeDtypeStruct((rows, LAT_W), BF16),
                 jax.ShapeDtypeStruct((rows, MOBA_Q_W), F32),
                 jax.ShapeDtypeStruct((rows, 2 * MOBA_KV_W), F32),
                 jax.ShapeDtypeStruct((rows, 2 * MOBA_KV_W), BF16)]
    out_specs = [pl.BlockSpec((MLA_HEADS, tm, LAT_W), lambda i: (0, i, 0)),
                 row(LAT_W), row(LAT_W), row(MOBA_Q_W), row(2 * MOBA_KV_W), row(2 * MOBA_KV_W)]
    if with_means:
        out_shape.append(jax.ShapeDtypeStruct((rows // tm, 1, MOBA_KV_W), F32))
        out_specs.append(pl.BlockSpec((1, 1, MOBA_KV_W), lambda i: (i, 0, 0)))
    return pl.pallas_call(
        functools.partial(_ab_kernel, with_means),
        grid=(rows // tm,),
        in_specs=[row(D_MODEL), _const_spec((1, D_MODEL)), _const_spec(wts['win'].shape),
                  _const_spec((1, Q_LORA)), _const_spec((1, KV_LORA)), _const_spec(wts['wq2'].shape),
                  _const_spec(wts['wukt'].shape), tab(384), tab(256)],
        out_specs=out_specs, out_shape=out_shape,
        compiler_params=_cparams("parallel"), name="ab_project",
    )(x, g_mix, wts['win'], wts['gq'], wts['gkv'], wts['wq2'], wts['wukt'], t64, t32)


def _mla_prefill_kernel(q_ref, lat_ref, wuv_ref, o_ref, m_sc, l_sc, acc_sc):
    tq, tk = MLA_TQ, MLA_TK
    qi = pl.program_id(1)
    q = q_ref[...].reshape(MLA_HEADS * tq, LAT_W)
    m_sc[...] = jnp.full(m_sc.shape, -jnp.inf, F32)
    l_sc[...] = jnp.zeros(l_sc.shape, F32)
    acc_sc[...] = jnp.zeros(acc_sc.shape, F32)
    q_pos = qi * tq + lax.broadcasted_iota(jnp.int32, (MLA_HEADS * tq, tk), 0) % tq
    col = lax.broadcasted_iota(jnp.int32, (MLA_HEADS * tq, tk), 1)

    def body(kt, carry):
        k = lat_ref[pl.ds(pl.multiple_of(kt * tk, tk), tk), :]
        s = lax.dot_general(q, k, _NT, preferred_element_type=F32)
        s = jnp.where(col + kt * tk <= q_pos, s, -jnp.inf)
        m, l, acc = _online_update(s, k[:, :KV_LORA], m_sc[...], l_sc[...], acc_sc[...])
        m_sc[...] = m
        l_sc[...] = l
        acc_sc[...] = acc
        return carry

    lax.fori_loop(0, (qi * tq) // tk + 1, body, 0)
    o_lat = (acc_sc[...] / l_sc[...]).astype(BF16)
    o_ref[...] = jnp.concatenate(
        [_bdot(o_lat[hh * tq:(hh + 1) * tq], wuv_ref[hh]) for hh in range(MLA_HEADS)], axis=-1).astype(o_ref.dtype)


def _mla_prefill(q, lat_bf, wuv, n_b, seq):
    tq = MLA_TQ
    nq = seq // tq
    rows = MLA_HEADS * tq
    return pl.pallas_call(
        _mla_prefill_kernel,
        grid=(n_b, nq),
        in_specs=[pl.BlockSpec((MLA_HEADS, tq, LAT_W), lambda b, i: (0, b * nq + i, 0)),
                  pl.BlockSpec((seq, LAT_W), lambda b, i: (b, 0)),
                  _const_spec(wuv.shape)],
        out_specs=pl.BlockSpec((tq, MLA_HEADS * MLA_V), lambda b, i: (b * nq + i, 0)),
        out_shape=jax.ShapeDtypeStruct((n_b * seq, MLA_HEADS * MLA_V), BF16),
        scratch_shapes=[pltpu.VMEM((rows, 1), F32), pltpu.VMEM((rows, 1), F32), pltpu.VMEM((rows, KV_LORA), F32)],
        compiler_params=_cparams("parallel", "parallel"), name="mla_prefill",
    )(q, lat_bf, wuv)


def _top_blocks(gate, valid):
    nb = gate.shape[1]
    idx = lax.broadcasted_iota(jnp.int32, gate.shape, 1)
    g = jnp.where(valid, gate, -jnp.inf)
    sel = jnp.zeros(gate.shape, F32)
    for _ in range(MOBA_TOPK):
        mx = jnp.max(g, axis=1, keepdims=True)
        cand = jnp.where(g == mx, idx, nb)
        first = jnp.min(cand, axis=1, keepdims=True)
        hit = (idx == first) & (mx > -jnp.inf)
        sel = jnp.where(hit, 1.0, sel)
        g = jnp.where(idx == first, -jnp.inf, g)
    return sel


def _gate_dot(q, means):
    return lax.dot_general(q, means, _NT, precision=lax.Precision.HIGHEST, preferred_element_type=F32)


def _moba_prefill_kernel(q_ref, kv_ref, means_ref, o_ref, m_sc, l_sc, acc_sc):
    blk = MOBA_BLOCK
    cur = pl.program_id(1)
    nb = means_ref.shape[1]
    rows = 2 * blk
    blk_idx = lax.broadcasted_iota(jnp.int32, (rows, nb), 1)
    r_in = lax.broadcasted_iota(jnp.int32, (rows, blk), 0) % blk
    c_in = lax.broadcasted_iota(jnp.int32, (rows, blk), 1)
    start = pl.multiple_of(cur * blk, blk)
    for kg in range(MOBA_KV_HEADS):
        k_lanes = pl.ds(HEAD_DIM * kg, HEAD_DIM)
        v_lanes = pl.ds(MOBA_KV_W + HEAD_DIM * kg, HEAD_DIM)
        qf = q_ref[:, 2 * HEAD_DIM * kg:2 * HEAD_DIM * (kg + 1)]
        q2 = jnp.concatenate([qf[:, :HEAD_DIM], qf[:, HEAD_DIM:]], axis=0)
        gate = _gate_dot(q2, means_ref[0][:, HEAD_DIM * kg:HEAD_DIM * (kg + 1)])
        sel = _top_blocks(gate, blk_idx < cur)
        q2b = (q2 * HEAD_SCALE).astype(BF16)
        s = lax.dot_general(q2b, kv_ref[pl.ds(start, blk), k_lanes], _NT, preferred_element_type=F32)
        s = jnp.where(c_in <= r_in, s, -jnp.inf)
        m0 = jnp.max(s, axis=1, keepdims=True)
        p = jnp.exp(s - m0)
        m_sc[...] = m0
        l_sc[...] = jnp.sum(p, axis=1, keepdims=True)
        acc_sc[...] = _bdot(p.astype(BF16), kv_ref[pl.ds(start, blk), v_lanes])

        def body(n, carry):
            off = pl.multiple_of(n * blk, blk)
            picked = jnp.sum(jnp.where(blk_idx == n, sel, 0.0), axis=1, keepdims=True)
            s = lax.dot_general(q2b, kv_ref[pl.ds(off, blk), k_lanes], _NT, preferred_element_type=F32)
            s = jnp.where(picked > 0.5, s, -jnp.inf)
            m, l, acc = _online_update(s, kv_ref[pl.ds(off, blk), v_lanes], m_sc[...], l_sc[...], acc_sc[...])
            m_sc[...] = m
            l_sc[...] = l
            acc_sc[...] = acc
            return carry

        lax.fori_loop(0, cur, body, 0)
        o = (acc_sc[...] / l_sc[...]).astype(o_ref.dtype)
        o_ref[:, 2 * HEAD_DIM * kg:2 * HEAD_DIM * kg + HEAD_DIM] = o[:blk]
        o_ref[:, 2 * HEAD_DIM * kg + HEAD_DIM:2 * HEAD_DIM * (kg + 1)] = o[blk:]


def _moba_prefill(qb, kvb_bf, means, n_b, seq):
    blk = MOBA_BLOCK
    nb = seq // blk
    return pl.pallas_call(
        _moba_prefill_kernel,
        grid=(n_b, nb),
        in_specs=[pl.BlockSpec((blk, MOBA_Q_W), lambda b, i: (b * nb + i, 0)),
                  pl.BlockSpec((seq, 2 * MOBA_KV_W), lambda b, i: (b, 0)),
                  pl.BlockSpec((1, nb, MOBA_KV_W), lambda b, i: (b, 0, 0))],
        out_specs=pl.BlockSpec((blk, MOBA_Q_W), lambda b, i: (b * nb + i, 0)),
        out_shape=jax.ShapeDtypeStruct((n_b * seq, MOBA_Q_W), BF16),
        scratch_shapes=[pltpu.VMEM((2 * blk, 1), F32), pltpu.VMEM((2 * blk, 1), F32),
                        pltpu.VMEM((2 * blk, HEAD_DIM), F32)],
        compiler_params=_cparams("parallel", "parallel"), name="moba_prefill",
    )(qb, kvb_bf, means)


def _out_proj_kernel(n_in, x_ref, *refs):
    a_refs, w_refs, o_ref = refs[:n_in], refs[n_in:2 * n_in], refs[2 * n_in]
    y = x_ref[...]
    for a_ref, w_ref in zip(a_refs, w_refs):
        y = y + _bdot(a_ref[...].astype(BF16), w_ref[...])
    o_ref[...] = y


def _out_proj(x, acts, ws):
    rows = x.shape[0]
    tm = ROW_TILE
    row = lambda w: pl.BlockSpec((tm, w), lambda i: (i, 0))
    return pl.pallas_call(
        functools.partial(_out_proj_kernel, len(acts)),
        grid=(rows // tm,),
        in_specs=[row(D_MODEL)] + [row(a.shape[1]) for a in acts] + [_const_spec(w.shape) for w in ws],
        out_specs=row(D_MODEL), out_shape=jax.ShapeDtypeStruct((rows, D_MODEL), F32),
        compiler_params=_cparams("parallel"), name="out_proj",
    )(x, *acts, *ws)


def _dil_merge_proj_kernel(x_ref, o0, o1, o2, l0, l1, l2, w_ref, out_ref):
    lses = [l0[...], l1[...], l2[...]]
    outs = [o0[...], o1[...], o2[...]]
    mx = jnp.maximum(jnp.maximum(lses[0], lses[1]), lses[2])
    ws = [jnp.exp(l - mx) for l in lses]
    tot = ws[0] + ws[1] + ws[2]
    o = (ws[0] * outs[0] + ws[1] * outs[1] + ws[2] * outs[2]) / tot
    out_ref[...] = x_ref[...] + _bdot(o.astype(BF16), w_ref[...])


def _dil_merge_proj(x, outs, lses, w):
    rows = x.shape[0]
    tm = ROW_TILE
    row = lambda wd: pl.BlockSpec((tm, wd), lambda i: (i, 0))
    return pl.pallas_call(
        _dil_merge_proj_kernel,
        grid=(rows // tm,),
        in_specs=[row(D_MODEL)] + [row(DIL_W)] * 6 + [_const_spec(w.shape)],
        out_specs=row(D_MODEL), out_shape=jax.ShapeDtypeStruct((rows, D_MODEL), F32),
        compiler_params=_cparams("parallel"), name="dil_merge_proj",
    )(x, *outs, *lses, w)


def _c_kernel(x_ref, gm_ref, w_ref, t64_ref, q_ref, kv_ref):
    h = _rms(x_ref[...], gm_ref[...]).astype(BF16)
    z = _bdot(h, w_ref[...])
    t64 = t64_ref[...]
    q_w = N_DIL * DIL_W
    q_ref[...] = _rope64(z[:, :q_w], t64)
    for g in range(N_DIL):
        base = q_w + 2 * DIL_W * g
        kv_ref[:, 2 * DIL_W * g:2 * DIL_W * g + DIL_W] = _rope64(z[:, base:base + DIL_W], t64)
        kv_ref[:, 2 * DIL_W * g + DIL_W:2 * DIL_W * (g + 1)] = z[:, base + DIL_W:base + 2 * DIL_W]


def _c_project(x, g_mix, w, t64):
    rows = x.shape[0]
    tm = ROW_TILE
    n_tab = t64.shape[0] // tm
    row = lambda wd: pl.BlockSpec((tm, wd), lambda i: (i, 0))
    return pl.pallas_call(
        _c_kernel,
        grid=(rows // tm,),
        in_specs=[row(D_MODEL), _const_spec((1, D_MODEL)), _const_spec(w.shape),
                  pl.BlockSpec((tm, 384), lambda i: (i % n_tab, 0))],
        out_specs=[row(N_DIL * DIL_W), row(2 * N_DIL * DIL_W)],
        out_shape=[jax.ShapeDtypeStruct((rows, N_DIL * DIL_W), F32),
                   jax.ShapeDtypeStruct((rows, 2 * N_DIL * DIL_W), F32)],
        compiler_params=_cparams("parallel"), name="c_project",
    )(x, g_mix, w, t64)


def _dil_prefill_kernel(q_ref, kp_ref, kc_ref, vp_ref, vc_ref, o_ref, lse_ref):
    tq = DIL_TQ
    i = pl.program_id(2)
    q = (q_ref[0] * HEAD_SCALE).astype(BF16)
    k2 = jnp.concatenate([kp_ref[0], kc_ref[0]], axis=0).astype(BF16)
    v2 = jnp.concatenate([vp_ref[0], vc_ref[0]], axis=0).astype(BF16)
    row = lax.broadcasted_iota(jnp.int32, (tq, 2 * tq), 0)
    col = lax.broadcasted_iota(jnp.int32, (tq, 2 * tq), 1)
    dist = row + tq - col
    first_col = jnp.where(i > 0, 0, tq)
    valid = (dist >= 0) & (dist <= DIL_KEYS) & (col >= first_col)
    outs, lses = [], []
    for hh in range(DIL_HEADS):
        lanes = slice(HEAD_DIM * hh, HEAD_DIM * (hh + 1))
        s = lax.dot_general(q[:, lanes], k2[:, lanes], _NT, preferred_element_type=F32)
        s = jnp.where(valid, s, -jnp.inf)
        m = jnp.max(s, axis=1, keepdims=True)
        p = jnp.exp(s - m)
        l = jnp.sum(p, axis=1, keepdims=True)
        outs.append(_bdot(p.astype(BF16), v2[:, lanes]) / l)
        lses.append(jnp.broadcast_to(m + jnp.log(l), (tq, HEAD_DIM)))
    o_ref[0] = jnp.concatenate(outs, axis=-1)
    lse_ref[0] = jnp.concatenate(lses, axis=-1)


def _dil_prefill(q, kv, g, n_b, seq):
    dil = DIL_CFG[g][1]
    tq = DIL_TQ
    s_rows = seq // dil
    nt = s_rows // tq
    qv = q.reshape(n_b, s_rows, dil * N_DIL * DIL_W)
    kvv = kv.reshape(n_b, s_rows, dil * 2 * N_DIL * DIL_W)
    blk = (1, tq, DIL_W)
    kcol = lambda r: r * 2 * N_DIL + 2 * g
    out, lse = pl.pallas_call(
        _dil_prefill_kernel,
        grid=(n_b, dil, nt),
        in_specs=[pl.BlockSpec(blk, lambda b, r, i: (b, i, r * N_DIL + g)),
                  pl.BlockSpec(blk, lambda b, r, i: (b, jnp.maximum(i - 1, 0), kcol(r))),
                  pl.BlockSpec(blk, lambda b, r, i: (b, i, kcol(r))),
                  pl.BlockSpec(blk, lambda b, r, i: (b, jnp.maximum(i - 1, 0), kcol(r) + 1)),
                  pl.BlockSpec(blk, lambda b, r, i: (b, i, kcol(r) + 1))],
        out_specs=[pl.BlockSpec(blk, lambda b, r, i: (b, i, r))] * 2,
        out_shape=[jax.ShapeDtypeStruct((n_b, s_rows, dil * DIL_W), F32)] * 2,
        compiler_params=_cparams("parallel", "parallel", "parallel"), name=f"dil_prefill_g{g}",
    )(qv, kvv, kvv, kvv, kvv)
    return out.reshape(n_b * seq, DIL_W), lse.reshape(n_b * seq, DIL_W)


def _ffn_kernel(mode, steps_per_seq, x_ref, g_ref, wup_ref, wc_ref, bc_ref, wd_ref, *refs):
    tm = x_ref.shape[0]
    if mode == "long":
        o_ref, tail_ref, ext = refs
    else:
        e1_ref, e2_ref, o_ref, tail_ref, ext = refs
    x = x_ref[...]
    h = _rms(x, g_ref[...]).astype(BF16)
    gu = _bdot(h, wup_ref[...])
    gate, up = gu[:, :D_FF], gu[:, D_FF:]
    if mode == "long":
        @pl.when(pl.program_id(0) % steps_per_seq == 0)
        def _():
            ext[0:8, :] = jnp.zeros((8, D_FF), F32)
    else:
        ext[0:8, :] = jnp.zeros((8, D_FF), F32)
    ext[8:8 + tm, :] = gate
    g1 = ext[7:7 + tm, :]
    g2 = ext[6:6 + tm, :]
    if mode == "long":
        ext[0:8, :] = gate[tm - 8:, :]
        tail_ref[0] = gate[tm - 8:, :]
    else:
        t = lax.broadcasted_iota(jnp.int32, (tm, D_FF), 0) % 8
        g1 = jnp.where(t < 1, e1_ref[...], g1)
        g2 = jnp.where(t < 2, e2_ref[...], g2)
        tail_ref[...] = gate
    wc = wc_ref[...]
    conv = bc_ref[...] + wc[0:1] * g2 + wc[1:2] * g1 + wc[2:3] * gate
    hid = (conv * jax.nn.sigmoid(conv) * up).astype(BF16)
    o_ref[...] = x + _bdot(hid, wd_ref[...])


def _conv_ffn(x, g, wup, wc, bc, wd, seq, hist=None):
    rows = x.shape[0]
    tm = ROW_TILE
    n_seq = rows // seq
    row = lambda w: pl.BlockSpec((tm, w), lambda i: (i, 0))
    consts = [_const_spec((1, D_MODEL)), _const_spec(wup.shape), _const_spec(wc.shape),
              _const_spec((1, D_FF)), _const_spec(wd.shape)]
    if hist is None:
        steps = seq // tm
        y, tail = pl.pallas_call(
            functools.partial(_ffn_kernel, "long", steps),
            grid=(rows // tm,),
            in_specs=[row(D_MODEL)] + consts,
            out_specs=[row(D_MODEL), pl.BlockSpec((1, 8, D_FF), lambda i: (i // steps, 0, 0))],
            out_shape=[jax.ShapeDtypeStruct((rows, D_MODEL), F32), jax.ShapeDtypeStruct((n_seq, 8, D_FF), F32)],
            scratch_shapes=[pltpu.VMEM((tm + 8, D_FF), F32)],
            compiler_params=_cparams("arbitrary"), name="conv_ffn_prompt",
        )(x, g, wup, wc, bc, wd)
        return y, tail[:, 8 - (CONV_W - 1):]
    assert seq == 8 and hist.shape == (n_seq, CONV_W - 1, D_FF)
    zeros = jnp.zeros((n_seq, 8 - (CONV_W - 1), D_FF), F32)
    e2 = jnp.concatenate([hist, zeros], axis=1).reshape(rows, D_FF)
    e1 = jnp.concatenate([hist[:, 1:], zeros, zeros[:, :1]], axis=1).reshape(rows, D_FF)
    y, gate = pl.pallas_call(
        functools.partial(_ffn_kernel, "short", 1),
        grid=(rows // tm,),
        in_specs=[row(D_MODEL)] + consts + [row(D_FF), row(D_FF)],
        out_specs=[row(D_MODEL), row(D_FF)],
        out_shape=[jax.ShapeDtypeStruct((rows, D_MODEL), F32), jax.ShapeDtypeStruct((rows, D_FF), F32)],
        scratch_shapes=[pltpu.VMEM((tm + 8, D_FF), F32)],
        compiler_params=_cparams("arbitrary"), name="conv_ffn_sample",
    )(x, g, wup, wc, bc, wd, e1, e2)
    return y, gate.reshape(n_seq, seq, D_FF)[:, seq - (CONV_W - 1):]


def _mla_decode_kernel(pt_ref, q_ref, new_ref, wuv_ref, *refs):
    pps = PAGES_PER_STEP
    pages, (o_ref, kbuf, m_sc, l_sc, acc_sc) = refs[:pps], refs[pps:]
    c = pl.program_id(1)
    rows = MLA_HEADS * 8
    q = q_ref[...].reshape(rows, LAT_W).astype(BF16)

    @pl.when(c == 0)
    def _():
        k = jnp.concatenate([new_ref[...], jnp.zeros((PAGE_SIZE - 8, LAT_W), F32)], axis=0).astype(BF16)
        s = lax.dot_general(q, k, _NT, preferred_element_type=F32)
        t = lax.broadcasted_iota(jnp.int32, s.shape, 0) % 8
        col = lax.broadcasted_iota(jnp.int32, s.shape, 1)
        s = jnp.where(col <= t, s, -jnp.inf)
        m = jnp.max(s, axis=1, keepdims=True)
        p = jnp.exp(s - m)
        m_sc[...] = m
        l_sc[...] = jnp.sum(p, axis=1, keepdims=True)
        acc_sc[...] = _bdot(p.astype(BF16), k[:, :KV_LORA])

    for j in range(pps):
        kbuf[PAGE_SIZE * j:PAGE_SIZE * (j + 1), :] = pages[j][0].astype(BF16)
    k = kbuf[...]
    s = lax.dot_general(q, k, _NT, preferred_element_type=F32)
    m, l, acc = _online_update(s, k[:, :KV_LORA], m_sc[...], l_sc[...], acc_sc[...])
    m_sc[...] = m
    l_sc[...] = l
    acc_sc[...] = acc

    @pl.when(c == pl.num_programs(1) - 1)
    def _():
        o_lat = (acc_sc[...] / l_sc[...]).astype(BF16)
        o_ref[...] = jnp.concatenate(
            [_bdot(o_lat[8 * hh:8 * (hh + 1)], wuv_ref[hh]) for hh in range(MLA_HEADS)], axis=-1)


def _mla_decode(q, lat_new, wuv, pool, page_table):
    n_b, n_pages = page_table.shape
    pps = PAGES_PER_STEP
    steps = n_pages // pps
    rows = MLA_HEADS * 8
    page = lambda j: pl.BlockSpec((1, PAGE_SIZE, LAT_W), lambda b, c, pt: (pt[b, c * pps + j], 0, 0))
    return pl.pallas_call(
        _mla_decode_kernel,
        grid_spec=pltpu.PrefetchScalarGridSpec(
            num_scalar_prefetch=1, grid=(n_b, steps),
            in_specs=[pl.BlockSpec((MLA_HEADS, 8, LAT_W), lambda b, c, pt: (0, b, 0)),
                      pl.BlockSpec((8, LAT_W), lambda b, c, pt: (b, 0)),
                      _const_spec(wuv.shape)] + [page(j) for j in range(pps)],
            out_specs=pl.BlockSpec((8, MLA_HEADS * MLA_V), lambda b, c, pt: (b, 0)),
            scratch_shapes=[pltpu.VMEM((pps * PAGE_SIZE, LAT_W), BF16), pltpu.VMEM((rows, 1), F32),
                            pltpu.VMEM((rows, 1), F32), pltpu.VMEM((rows, KV_LORA), F32)]),
        out_shape=jax.ShapeDtypeStruct((n_b * 8, MLA_HEADS * MLA_V), F32),
        compiler_params=_cparams("parallel", "arbitrary"), name="mla_decode",
    )(page_table, q, lat_new, wuv, *([pool] * pps))


def _moba_decode_kernel(pt_ref, q_ref, new_ref, *refs):
    pps = PAGES_PER_STEP
    pages, (o_ref, kvbuf, means_sc) = refs[:pps], refs[pps:]
    c = pl.program_id(1)
    n_steps = pl.num_programs(1)
    chunk = pps * PAGE_SIZE
    blocks_per_step = chunk // MOBA_BLOCK
    pages_per_block = MOBA_BLOCK // PAGE_SIZE
    base = pl.multiple_of(c * chunk, chunk)
    sums = []
    for j in range(pps):
        pg = pages[j][0]
        kvbuf[pl.ds(base + PAGE_SIZE * j, PAGE_SIZE), :] = pg.astype(BF16)
        sums.append(jnp.sum(pg[:, :MOBA_KV_W], axis=0, keepdims=True))
    means = [sum(sums[pages_per_block * jb:pages_per_block * (jb + 1)]) * (1.0 / MOBA_BLOCK)
             for jb in range(blocks_per_step)]
    means_sc[pl.ds(pl.multiple_of(c * blocks_per_step, blocks_per_step), blocks_per_step), :] = (
        jnp.concatenate(means, axis=0))

    @pl.when(c == n_steps - 1)
    def _():
        rows = MOBA_HEADS * 8
        nb = means_sc.shape[0]
        q = q_ref[...]
        grp = MOBA_HEADS // MOBA_KV_HEADS
        pieces = []
        for hh in range(MOBA_HEADS):
            kvh = hh // grp
            parts = []
            if kvh > 0:
                parts.append(jnp.zeros((8, HEAD_DIM * kvh), F32))
            parts.append(q[:, HEAD_DIM * hh:HEAD_DIM * (hh + 1)])
            if kvh < MOBA_KV_HEADS - 1:
                parts.append(jnp.zeros((8, HEAD_DIM * (MOBA_KV_HEADS - 1 - kvh)), F32))
            pieces.append(jnp.concatenate(parts, axis=-1))
        qbd = jnp.concatenate(pieces, axis=0)
        sel = _top_blocks(_gate_dot(qbd, means_sc[...]), jnp.full((rows, nb), True))
        qb = (qbd * HEAD_SCALE).astype(BF16)
        new = jnp.concatenate([new_ref[...], jnp.zeros((PAGE_SIZE - 8, 2 * MOBA_KV_W), F32)], axis=0).astype(BF16)
        s = lax.dot_general(qb, new[:, :MOBA_KV_W], _NT, preferred_element_type=F32)
        t = lax.broadcasted_iota(jnp.int32, s.shape, 0) % 8
        col = lax.broadcasted_iota(jnp.int32, s.shape, 1)
        s = jnp.where(col <= t, s, -jnp.inf)
        m = jnp.max(s, axis=1, keepdims=True)
        p = jnp.exp(s - m)
        l = jnp.sum(p, axis=1, keepdims=True)
        acc = _bdot(p.astype(BF16), new[:, MOBA_KV_W:])
        selb = sel.astype(BF16)
        for ch in range(kvbuf.shape[0] // chunk):
            kv = kvbuf[chunk * ch:chunk * (ch + 1), :]
            s = lax.dot_general(qb, kv[:, :MOBA_KV_W], _NT, preferred_element_type=F32)
            blk_of_col = lax.broadcasted_iota(jnp.int32, (nb, chunk), 1) // MOBA_BLOCK + ch * blocks_per_step
            expand = jnp.where(lax.broadcasted_iota(jnp.int32, (nb, chunk), 0) == blk_of_col, 1.0, 0.0).astype(BF16)
            s = jnp.where(_bdot(selb, expand) > 0.5, s, -jnp.inf)
            m, l, acc = _online_update(s, kv[:, MOBA_KV_W:], m, l, acc)
        o = acc / l
        o_ref[...] = jnp.concatenate(
            [o[8 * hh:8 * (hh + 1), HEAD_DIM * (hh // grp):HEAD_DIM * (hh // grp + 1)] for hh in range(MOBA_HEADS)],
            axis=-1)


def _moba_decode(qb, kv_new, pool, page_table):
    n_b, n_pages = page_table.shape
    pps = PAGES_PER_STEP
    steps = n_pages // pps
    past = n_pages * PAGE_SIZE
    page = lambda j: pl.BlockSpec((1, PAGE_SIZE, 2 * MOBA_KV_W), lambda b, c, pt: (pt[b, c * pps + j], 0, 0))
    return pl.pallas_call(
        _moba_decode_kernel,
        grid_spec=pltpu.PrefetchScalarGridSpec(
            num_scalar_prefetch=1, grid=(n_b, steps),
            in_specs=[pl.BlockSpec((8, MOBA_Q_W), lambda b, c, pt: (b, 0)),
                      pl.BlockSpec((8, 2 * MOBA_KV_W), lambda b, c, pt: (b, 0))] + [page(j) for j in range(pps)],
            out_specs=pl.BlockSpec((8, MOBA_Q_W), lambda b, c, pt: (b, 0)),
            scratch_shapes=[pltpu.VMEM((past, 2 * MOBA_KV_W), BF16),
                            pltpu.VMEM((past // MOBA_BLOCK, MOBA_KV_W), F32)]),
        out_shape=jax.ShapeDtypeStruct((n_b * 8, MOBA_Q_W), F32),
        compiler_params=_cparams("parallel", "arbitrary"), name="moba_decode",
    )(page_table, qb, kv_new, *([pool] * pps))


def _dil_decode_kernel(past_len, q_ref, new_ref, st_ref, nxt_ref, o_ref, win_ref, m_sc, l_sc, acc_sc):
    c = pl.program_id(1)
    n_steps = pl.num_programs(1)
    rows = DIL_HEADS * 8
    chunk = WIN_CHUNK
    starts = [sum(w for w, _ in DIL_CFG[:g]) for g in range(N_DIL)]

    q = q_ref[...] * HEAD_SCALE
    r_head = lax.broadcasted_iota(jnp.int32, (rows, DIL_W), 0) // 8
    l_head = lax.broadcasted_iota(jnp.int32, (rows, DIL_W), 1) // HEAD_DIM
    q_all = jnp.concatenate(
        [jnp.where(r_head == l_head, jnp.tile(q[:, DIL_W * g:DIL_W * (g + 1)], (DIL_HEADS, 1)), 0.0)
         for g in range(N_DIL)], axis=0).astype(BF16)
    new = new_ref[...]

    def scores(k_bf, grp_of_col):
        s_all = lax.dot_general(q_all, k_bf, _NT, preferred_element_type=F32)
        return jnp.where(grp_of_col == 0, s_all[0:rows],
                         jnp.where(grp_of_col == 1, s_all[rows:2 * rows], s_all[2 * rows:3 * rows]))

    @pl.when(c == 0)
    def _():
        kn = jnp.concatenate([new[:, 2 * DIL_W * g:2 * DIL_W * g + DIL_W] for g in range(N_DIL)]
                             + [jnp.zeros((PAGE_SIZE - 8 * N_DIL, DIL_W), F32)], axis=0).astype(BF16)
        vn = jnp.concatenate([new[:, 2 * DIL_W * g + DIL_W:2 * DIL_W * (g + 1)] for g in range(N_DIL)]
                             + [jnp.zeros((PAGE_SIZE - 8 * N_DIL, DIL_W), F32)], axis=0).astype(BF16)
        col = lax.broadcasted_iota(jnp.int32, (rows, PAGE_SIZE), 1)
        t = lax.broadcasted_iota(jnp.int32, (rows, PAGE_SIZE), 0) % 8
        grp = col // 8
        dist = t - col % 8
        dmask = jnp.where(grp == 0, DIL_CFG[0][1] - 1, jnp.where(grp == 1, DIL_CFG[1][1] - 1, DIL_CFG[2][1] - 1))
        ok = (grp < N_DIL) & (dist >= 0) & ((dist & dmask) == 0)
        s = jnp.where(ok, scores(kn, grp), -jnp.inf)
        m = jnp.max(s, axis=1, keepdims=True)
        p = jnp.exp(s - m)
        m_sc[...] = m
        l_sc[...] = jnp.sum(p, axis=1, keepdims=True)
        acc_sc[...] = _bdot(p.astype(BF16), vn)

    st = st_ref[0]
    e_glob = c * chunk + lax.broadcasted_iota(jnp.int32, (rows, chunk), 1)
    t = lax.broadcasted_iota(jnp.int32, (rows, chunk), 0) % 8
    grp = (e_glob >= starts[1]).astype(jnp.int32) + (e_glob >= starts[2]).astype(jnp.int32)
    pick = lambda vals: jnp.where(grp == 0, vals[0], jnp.where(grp == 1, vals[1], vals[2]))
    win = pick([w for w, _ in DIL_CFG])
    dist = win + t - (e_glob - pick(starts))
    ok = (dist <= win) & ((dist & pick([d - 1 for _, d in DIL_CFG])) == 0)
    if past_len < max(w for w, _ in DIL_CFG):
        ok = ok & (dist <= past_len + t)
    s = jnp.where(ok, scores(st[:, :DIL_W].astype(BF16), grp), -jnp.inf)
    m, l, acc = _online_update(s, st[:, DIL_W:].astype(BF16), m_sc[...], l_sc[...], acc_sc[...])
    m_sc[...] = m
    l_sc[...] = l
    acc_sc[...] = acc

    win_ref[0, 0:chunk - 8, :] = st[8:, :]
    win_ref[0, chunk - 8:chunk, :] = nxt_ref[0]
    for g in range(N_DIL):
        end = starts[g] + DIL_CFG[g][0]
        step, off = (end - 8) // chunk, (end - 8) % chunk

        @pl.when(c == step)
        def _(g=g, off=off):
            win_ref[0, off:off + 8, :] = new[:, 2 * DIL_W * g:2 * DIL_W * (g + 1)]

    @pl.when(c == n_steps - 1)
    def _():
        o = acc_sc[...] / l_sc[...]
        o_ref[...] = jnp.concatenate(
            [o[8 * hh:8 * (hh + 1), HEAD_DIM * hh:HEAD_DIM * (hh + 1)] for hh in range(DIL_HEADS)], axis=-1)


def _dil_decode(q, kv_new, state, past_len):
    n_b = state.shape[0]
    chunk = WIN_CHUNK
    steps = WIN_ROWS // chunk
    st = state.reshape(n_b, WIN_ROWS, 2 * DIL_W)
    rows = DIL_HEADS * 8
    last8 = WIN_ROWS // 8 - 1
    o, win = pl.pallas_call(
        functools.partial(_dil_decode_kernel, past_len),
        grid=(n_b, steps),
        in_specs=[pl.BlockSpec((8, N_DIL * DIL_W), lambda b, c: (b, 0)),
                  pl.BlockSpec((8, 2 * N_DIL * DIL_W), lambda b, c: (b, 0)),
                  pl.BlockSpec((1, chunk, 2 * DIL_W), lambda b, c: (b, c, 0)),
                  pl.BlockSpec((1, 8, 2 * DIL_W), lambda b, c: (b, jnp.minimum((c + 1) * (chunk // 8), last8), 0))],
        out_specs=[pl.BlockSpec((8, DIL_W), lambda b, c: (b, 0)),
                   pl.BlockSpec((1, chunk, 2 * DIL_W), lambda b, c: (b, c, 0))],
        out_shape=[jax.ShapeDtypeStruct((n_b * 8, DIL_W), F32),
                   jax.ShapeDtypeStruct((n_b, WIN_ROWS, 2 * DIL_W), F32)],
        scratch_shapes=[pltpu.VMEM((rows, 1), F32), pltpu.VMEM((rows, 1), F32), pltpu.VMEM((rows, DIL_W), F32)],
        compiler_params=_cparams("parallel", "arbitrary"), name="dil_decode",
    )(q, kv_new, st, st)
    return o, win.reshape(n_b, WIN_ROWS, 2, DIL_HEADS, HEAD_DIM)


def _final_norm_kernel(x_ref, g_ref, o_ref):
    o_ref[...] = _rms(x_ref[...], g_ref[...])


def _final_norm(x, g):
    rows = x.shape[0]
    tm = ROW_TILE
    row = pl.BlockSpec((tm, D_MODEL), lambda i: (i, 0))
    return pl.pallas_call(
        _final_norm_kernel, grid=(rows // tm,),
        in_specs=[row, _const_spec((1, D_MODEL))], out_specs=row,
        out_shape=jax.ShapeDtypeStruct((rows, D_MODEL), F32),
        compiler_params=_cparams("parallel"), name="final_norm",
    )(x, g)


def _rope_tables(pos):
    n = pos.shape[0]
    posf = pos.astype(F32)[:, None]
    half = ROT_DIM // 2
    ang = posf * (ROPE_THETA ** (-jnp.arange(half, dtype=F32) / half))[None, :]
    c, s = jnp.cos(ang), jnp.sin(ang)
    rest = HEAD_DIM - ROT_DIM
    cos64 = jnp.concatenate([c, c, jnp.ones((n, rest), F32)], axis=1)
    sin_hi = jnp.concatenate([-s, jnp.zeros((n, half + rest), F32)], axis=1)
    sin_lo = jnp.concatenate([jnp.zeros((n, half), F32), s, jnp.zeros((n, rest), F32)], axis=1)
    t64 = jnp.concatenate([jnp.tile(cos64, (1, 2)), jnp.tile(sin_hi, (1, 2)), jnp.tile(sin_lo, (1, 2))], axis=1)
    half = MLA_ROPE // 2
    ang = posf * (ROPE_THETA ** (-jnp.arange(half, dtype=F32) / half))[None, :]
    c, s = jnp.cos(ang), jnp.sin(ang)
    t32 = jnp.concatenate([jnp.tile(jnp.concatenate([c, c], axis=1), (1, 4)),
                           jnp.tile(jnp.concatenate([s, s], axis=1), (1, 4))], axis=1)
    return t64, t32


def _rot_cols(w, half):
    return jnp.concatenate([-w[..., half:], w[..., :half]], axis=-1)


def _even_weights(w_in, g_q, g_kv, w_uq, w_uk, w_uv, w_o):
    o1 = Q_LORA
    o2 = o1 + KV_LORA
    o3 = o2 + MLA_ROPE
    w_kr = w_in[:, o2:o3]
    win = jnp.concatenate([w_in[:, o3:], w_in[:, :o1], w_in[:, o1:o2], w_kr, _rot_cols(w_kr, MLA_ROPE // 2)], axis=1)
    w_nope = w_uq[:, :, :MLA_NOPE].reshape(Q_LORA, MLA_HEADS * MLA_NOPE)
    w_rope = w_uq[:, :, MLA_NOPE:]
    wq2 = jnp.concatenate([w_nope, w_rope.reshape(Q_LORA, MLA_HEADS * MLA_ROPE),
                           _rot_cols(w_rope, MLA_ROPE // 2).reshape(Q_LORA, MLA_HEADS * MLA_ROPE)], axis=1)
    n_a = MLA_HEADS * MLA_V
    return dict(win=win.astype(BF16), gq=g_q[None, :], gkv=g_kv[None, :], wq2=wq2.astype(BF16),
                wukt=jnp.transpose(w_uk, (0, 2, 1)).astype(BF16), wuv=w_uv.astype(BF16),
                wo_a=w_o[:n_a].astype(BF16), wo_b=w_o[n_a:].astype(BF16))


def _odd_weights(w_qkv):
    q_w = N_DIL * DIL_W
    cols = [w_qkv[:, :q_w]]
    for g in range(N_DIL):
        cols.append(w_qkv[:, q_w + DIL_W * g:q_w + DIL_W * (g + 1)])
        cols.append(w_qkv[:, 2 * q_w + DIL_W * g:2 * q_w + DIL_W * (g + 1)])
    return jnp.concatenate(cols, axis=1).astype(BF16)


def kernel(x_prompt, x_sample, cache_mla_l0, cache_moba_l0, state_win_l1, cache_mla_l2, cache_moba_l2, state_win_l3, state_ffn_conv, page_table, g_mix, g_ffn, g_final, w_in_ab, g_q_lat, g_kv_lat, w_uq, w_uk, w_uv, w_o_ab, w_qkv_c, w_o_c, w_up, w_conv, b_conv, w_down):
    n_p, s_p = x_prompt.shape[:2]
    n_s, s_s = x_sample.shape[:2]
    n_pages = page_table.shape[1]
    past_len = n_pages * PAGE_SIZE
    depth = g_mix.shape[0]
    max_win = max(w for w, _ in DIL_CFG)
    assert s_s == 8 and s_p % max_win == 0 and past_len % (PAGES_PER_STEP * PAGE_SIZE) == 0
    assert past_len >= max_win and (n_s * s_s) % ROW_TILE == 0 and WIN_ROWS % WIN_CHUNK == 0
    assert s_p // MOBA_BLOCK >= MOBA_TOPK and past_len // MOBA_BLOCK >= MOBA_TOPK

    t64_p, t32_p = _rope_tables(jnp.arange(s_p))
    t64_s, t32_s = _rope_tables(jnp.tile(past_len + jnp.arange(s_s), n_s))
    mla_caches = (cache_mla_l0, cache_mla_l2)
    moba_caches = (cache_moba_l0, cache_moba_l2)
    win_states = (state_win_l1, state_win_l3)

    xp = x_prompt.reshape(n_p * s_p, D_MODEL)
    xs = x_sample.reshape(n_s * s_s, D_MODEL)
    lat_p, lat_s, kvb_p, kvb_s, win_p, win_s, conv_p, conv_s = [], [], [], [], [], [], [], []
    for layer in range(depth):
        i = layer // 2
        gm = g_mix[layer][None, :]
        if layer % 2 == 0:
            wts = _even_weights(w_in_ab[i], g_q_lat[i], g_kv_lat[i], w_uq[i], w_uk[i], w_uv[i], w_o_ab[i])
            q, lat, lat_bf, qb, kv, kv_bf, means = _ab_project(xp, gm, wts, t64_p, t32_p, BF16, True)
            oa = _mla_prefill(q, lat_bf, wts['wuv'], n_p, s_p)
            ob = _moba_prefill(qb, kv_bf, means.reshape(n_p, s_p // MOBA_BLOCK, MOBA_KV_W), n_p, s_p)
            xp = _out_proj(xp, [oa, ob], [wts['wo_a'], wts['wo_b']])
            lat_p.append(lat.reshape(n_p, s_p, LAT_W))
            kvb_p.append(kv.reshape(n_p, s_p, 2, MOBA_KV_HEADS, HEAD_DIM))
            q2, lat2, _, qb2, kv2, _ = _ab_project(xs, gm, wts, t64_s, t32_s, F32, False)
            oa2 = _mla_decode(q2, lat2, wts['wuv'], mla_caches[i], page_table)
            pool = moba_caches[i].reshape(moba_caches[i].shape[0], PAGE_SIZE, 2 * MOBA_KV_W)
            ob2 = _moba_decode(qb2, kv2, pool, page_table)
            xs = _out_proj(xs, [oa2, ob2], [wts['wo_a'], wts['wo_b']])
            lat_s.append(lat2.reshape(n_s, s_s, LAT_W))
            kvb_s.append(kv2.reshape(n_s, s_s, 2, MOBA_KV_HEADS, HEAD_DIM))
        else:
            wqkv = _odd_weights(w_qkv_c[i])
            wo = w_o_c[i].astype(BF16)
            qd, kvd = _c_project(xp, gm, wqkv, t64_p)
            parts = [_dil_prefill(qd, kvd, g, n_p, s_p) for g in range(N_DIL)]
            xp = _dil_merge_proj(xp, [o for o, _ in parts], [l for _, l in parts], wo)
            kv5 = kvd.reshape(n_p, s_p, N_DIL, 2, DIL_HEADS, HEAD_DIM)
            win_p.append(jnp.concatenate([kv5[:, s_p - w:, g] for g, (w, _) in enumerate(DIL_CFG)], axis=1))
            qd2, kvd2 = _c_project(xs, gm, wqkv, t64_s)
            oc2, win2 = _dil_decode(qd2, kvd2, win_states[i], past_len)
            xs = _out_proj(xs, [oc2], [wo])
            win_s.append(win2)
        wup = w_up[layer].astype(BF16)
        wd = w_down[layer].astype(BF16)
        gf = g_ffn[layer][None, :]
        bc = b_conv[layer][None, :]
        xp, cp = _conv_ffn(xp, gf, wup, w_conv[layer], bc, wd, s_p)
        xs, cs = _conv_ffn(xs, gf, wup, w_conv[layer], bc, wd, s_s, hist=state_ffn_conv[layer])
        conv_p.append(cp)
        conv_s.append(cs)
    gfin = g_final[None, :]
    y_prompt = _final_norm(xp, gfin).reshape(n_p, s_p, D_MODEL)
    y_sample = _final_norm(xs, gfin).reshape(n_s, s_s, D_MODEL)
    return (y_prompt, y_sample,
            lat_p[0], lat_s[0], kvb_p[0], kvb_s[0], win_p[0], win_s[0],
            lat_p[1], lat_s[1], kvb_p[1], kvb_s[1], win_p[1], win_s[1],
            jnp.stack(conv_p), jnp.stack(conv_s))
```

```python
import functools

import jax
import jax.numpy as jnp
from jax import lax
from jax.experimental import pallas as pl
from jax.experimental.pallas import tpu as pltpu

F32 = jnp.float32
BF16 = jnp.bfloat16

D_MODEL = 1024
PAGE_SIZE = 128
HEAD_DIM = 64
ROT_DIM = HEAD_DIM // 4
ROPE_THETA = 500000.0
NORM_EPS = 1e-6
HEAD_SCALE = HEAD_DIM ** -0.5

MLA_HEADS = 8
MLA_NOPE = 64
MLA_ROPE = 32
MLA_V = 64
Q_LORA = 384
KV_LORA = 256
LAT_W = KV_LORA + MLA_ROPE
MLA_SCALE = (MLA_NOPE + MLA_ROPE) ** -0.5

MOBA_HEADS = 8
MOBA_KV_HEADS = 4
MOBA_BLOCK = 256
MOBA_TOPK = 3
MOBA_Q_W = MOBA_HEADS * HEAD_DIM
MOBA_KV_W = MOBA_KV_HEADS * HEAD_DIM

DIL_CFG = ((128, 1), (512, 4), (2048, 16))
N_DIL = len(DIL_CFG)
DIL_HEADS = 8
DIL_W = DIL_HEADS * HEAD_DIM
DIL_KEYS = 128
WIN_ROWS = sum(w for w, _ in DIL_CFG)

D_FF = 2816
CONV_W = 3

VMEM_LIMIT = 56 * 1024 * 1024

ROW_TILE = 256
MLA_TQ = 128
MLA_TK = 512
DIL_TQ = 128
PAGES_PER_STEP = 16

_NT = (((1,), (1,)), ((), ()))
_TN = (((0,), (0,)), ((), ()))


def _cparams(*sem):
    return pltpu.CompilerParams(dimension_semantics=sem, vmem_limit_bytes=VMEM_LIMIT)


def _const_spec(shape):
    nd = len(shape)
    return pl.BlockSpec(shape, lambda *_: (0,) * nd, pipeline_mode=pl.Buffered(1))


def _rms(x, g):
    return x * lax.rsqrt(jnp.mean(x * x, axis=-1, keepdims=True) + NORM_EPS) * g


def _rope64(x, tab):
    w = x.shape[1]
    n = w // 128
    c = jnp.tile(tab[:, 0:128], (1, n))
    s1 = jnp.tile(tab[:, 128:256], (1, n))
    s2 = jnp.tile(tab[:, 256:384], (1, n))
    half = ROT_DIM // 2
    x_hi = pltpu.roll(x, w - half, 1)
    x_lo = pltpu.roll(x, half, 1)
    return x * c + x_hi * s1 + x_lo * s2


def _bdot(a, b):
    return jnp.dot(a, b, preferred_element_type=F32)


def _online_update(s, v, m, l, acc, v_transposed=False):
    m_new = jnp.maximum(m, jnp.max(s, axis=1, keepdims=True))
    alpha = jnp.exp(m - m_new)
    p = jnp.exp(s - m_new)
    l_new = alpha * l + jnp.sum(p, axis=1, keepdims=True)
    if v_transposed:
        pv = lax.dot_general(p.astype(BF16), v, _NT, preferred_element_type=F32)
    else:
        pv = _bdot(p.astype(BF16), v)
    return m_new, l_new, alpha * acc + pv


def _ab_kernel(with_means, x_ref, gm_ref, win_ref, gq_ref, gkv_ref, wq2_ref, wukt_ref, t64_ref, t32_ref,
               q_ref, lat_ref, latb_ref, qb_ref, kv_ref, kvb_ref, *mean_ref):
    h = _rms(x_ref[...], gm_ref[...]).astype(BF16)
    z = _bdot(h, win_ref[...])
    t64 = t64_ref[...]
    t32 = t32_ref[...]
    qk = _rope64(z[:, :MOBA_Q_W + MOBA_KV_W], t64)
    qb_ref[...] = qk[:, :MOBA_Q_W]
    kv = jnp.concatenate([qk[:, MOBA_Q_W:], z[:, MOBA_Q_W + MOBA_KV_W:1024]], axis=-1)
    kv_ref[...] = kv
    kvb_ref[...] = kv.astype(BF16)
    if with_means:
        mean_ref[0][0] = jnp.sum(qk[:, MOBA_Q_W:], axis=0, keepdims=True) * (1.0 / MOBA_BLOCK)
    zl = z[:, 1024 + Q_LORA:]
    ckv = _rms(zl[:, :KV_LORA], gkv_ref[...])
    kr = (zl[:, KV_LORA:KV_LORA + MLA_ROPE] * t32[:, 0:MLA_ROPE]
          + zl[:, KV_LORA + MLA_ROPE:] * t32[:, 128:128 + MLA_ROPE])
    lat = jnp.concatenate([ckv, kr], axis=-1)
    lat_ref[...] = lat
    latb_ref[...] = lat.astype(BF16)
    cq = _rms(z[:, 1024:1024 + Q_LORA], gq_ref[...]).astype(BF16)
    q2 = _bdot(cq, wq2_ref[...])
    c256 = jnp.tile(t32[:, 0:128], (1, 2))
    s256 = jnp.tile(t32[:, 128:256], (1, 2))
    qr = (q2[:, 512:768] * c256 + q2[:, 768:1024] * s256) * MLA_SCALE
    for hh in range(MLA_HEADS):
        ql = _bdot(q2[:, MLA_NOPE * hh:MLA_NOPE * (hh + 1)].astype(BF16), wukt_ref[hh]) * MLA_SCALE
        q_ref[hh, :, 0:KV_LORA] = ql.astype(q_ref.dtype)
        q_ref[hh, :, KV_LORA:LAT_W] = qr[:, MLA_ROPE * hh:MLA_ROPE * (hh + 1)].astype(q_ref.dtype)


def _ab_project(x, g_mix, wts, t64, t32, q_dtype, with_means):
    rows = x.shape[0]
    tm = ROW_TILE
    n_tab = t64.shape[0] // tm
    row = lambda w: pl.BlockSpec((tm, w), lambda i: (i, 0))
    tab = lambda w: pl.BlockSpec((tm, w), lambda i: (i % n_tab, 0))
    out_shape = [jax.ShapeDtypeStruct((MLA_HEADS, rows, LAT_W), q_dtype),
                 jax.ShapeDtypeStruct((rows, LAT_W), F32),
                 jax.ShapeDtypeStruct((rows, LAT_W), BF16),
                 jax.ShapeDtypeStruct((rows, MOBA_Q_W), F32),
                 jax.ShapeDtypeStruct((rows, 2 * MOBA_KV_W), F32),
                 jax.ShapeDtypeStruct((rows, 2 * MOBA_KV_W), BF16)]
    out_specs = [pl.BlockSpec((MLA_HEADS, tm, LAT_W), lambda i: (0, i, 0)),
                 row(LAT_W), row(LAT_W), row(MOBA_Q_W), row(2 * MOBA_KV_W), row(2 * MOBA_KV_W)]
    if with_means:
        out_shape.append(jax.ShapeDtypeStruct((rows // tm, 1, MOBA_KV_W), F32))
        out_specs.append(pl.BlockSpec((1, 1, MOBA_KV_W), lambda i: (i, 0, 0)))
    return pl.pallas_call(
        functools.partial(_ab_kernel, with_means),
        grid=(rows // tm,),
        in_specs=[row(D_MODEL), _const_spec((1, D_MODEL)), _const_spec(wts['win'].shape),
                  _const_spec((1, Q_LORA)), _const_spec((1, KV_LORA)), _const_spec(wts['wq2'].shape),
                  _const_spec(wts['wukt'].shape), tab(384), tab(256)],
        out_specs=out_specs, out_shape=out_shape,
        compiler_params=_cparams("parallel"), name="ab_project",
    )(x, g_mix, wts['win'], wts['gq'], wts['gkv'], wts['wq2'], wts['wukt'], t64, t32)


def _mla_prefill_kernel(q_ref, lat_ref, wuv_ref, o_ref, m_sc, l_sc, acc_sc):
    tq, tk = MLA_TQ, MLA_TK
    qi = pl.program_id(1)
    q = q_ref[...].reshape(MLA_HEADS * tq, LAT_W)
    m_sc[...] = jnp.full(m_sc.shape, -jnp.inf, F32)
    l_sc[...] = jnp.zeros(l_sc.shape, F32)
    acc_sc[...] = jnp.zeros(acc_sc.shape, F32)
    def step(kt, masked):
        k = lat_ref[pl.ds(pl.multiple_of(kt * tk, tk), tk), :]
        s = lax.dot_general(q, k, _NT, preferred_element_type=F32)
        if masked:
            q_pos = qi * tq + lax.broadcasted_iota(jnp.int32, s.shape, 0) % tq
            s = jnp.where(lax.broadcasted_iota(jnp.int32, s.shape, 1) + kt * tk <= q_pos, s, -jnp.inf)
        m, l, acc = _online_update(s, k[:, :KV_LORA], m_sc[...], l_sc[...], acc_sc[...])
        m_sc[...] = m
        l_sc[...] = l
        acc_sc[...] = acc

    def body(kt, carry):
        step(kt, False)
        return carry

    n_full = (qi * tq) // tk
    lax.fori_loop(0, n_full, body, 0)
    step(n_full, True)
    o_lat = (acc_sc[...] / l_sc[...]).astype(BF16)
    o_ref[...] = jnp.concatenate(
        [_bdot(o_lat[hh * tq:(hh + 1) * tq], wuv_ref[hh]) for hh in range(MLA_HEADS)], axis=-1).astype(o_ref.dtype)


def _mla_prefill(q, lat_bf, wuv, n_b, seq):
    tq = MLA_TQ
    nq = seq // tq
    rows = MLA_HEADS * tq
    return pl.pallas_call(
        _mla_prefill_kernel,
        grid=(n_b, nq),
        in_specs=[pl.BlockSpec((MLA_HEADS, tq, LAT_W), lambda b, i: (0, b * nq + i, 0)),
                  pl.BlockSpec((seq, LAT_W), lambda b, i: (b, 0)),
                  _const_spec(wuv.shape)],
        out_specs=pl.BlockSpec((tq, MLA_HEADS * MLA_V), lambda b, i: (b * nq + i, 0)),
        out_shape=jax.ShapeDtypeStruct((n_b * seq, MLA_HEADS * MLA_V), BF16),
        scratch_shapes=[pltpu.VMEM((rows, 1), F32), pltpu.VMEM((rows, 1), F32), pltpu.VMEM((rows, KV_LORA), F32)],
        compiler_params=_cparams("parallel", "parallel"), name="mla_prefill",
    )(q, lat_bf, wuv)


def _top_blocks(gate, valid, axis):
    n = gate.shape[axis]
    idx = lax.broadcasted_iota(jnp.int32, gate.shape, axis)
    g = jnp.where(valid, gate, -jnp.inf)
    sel = jnp.zeros(gate.shape, F32)
    for _ in range(MOBA_TOPK):
        mx = jnp.max(g, axis=axis, keepdims=True)
        first = jnp.min(jnp.where(g == mx, idx, n), axis=axis, keepdims=True)
        sel = jnp.where((idx == first) & (mx > -jnp.inf), 1.0, sel)
        g = jnp.where(idx == first, -jnp.inf, g)
    return sel


def _moba_prefill_kernel(q_ref, kv_ref, means_ref, o_ref, sel_sc, acc_sc):
    blk = MOBA_BLOCK
    cur = pl.program_id(1)
    nb = means_ref.shape[1]
    cols = 2 * blk
    start = pl.multiple_of(cur * blk, blk)
    q_t = q_ref[...].T
    blk_idx = lax.broadcasted_iota(jnp.int32, (nb, cols), 0)
    key_in = lax.broadcasted_iota(jnp.int32, (blk, cols), 0)
    qry_in = lax.broadcasted_iota(jnp.int32, (blk, cols), 1) % blk
    for kg in range(MOBA_KV_HEADS):
        k_lanes = pl.ds(HEAD_DIM * kg, HEAD_DIM)
        v_lanes = pl.ds(MOBA_KV_W + HEAD_DIM * kg, HEAD_DIM)
        base = 2 * HEAD_DIM * kg
        q2 = jnp.concatenate([q_t[base:base + HEAD_DIM], q_t[base + HEAD_DIM:base + 2 * HEAD_DIM]], axis=1)
        gate = jnp.dot(means_ref[0][:, HEAD_DIM * kg:HEAD_DIM * (kg + 1)], q2,
                       precision=lax.Precision.HIGHEST, preferred_element_type=F32)
        sel_sc[...] = _top_blocks(gate, blk_idx < cur, 0)
        q2b = (q2 * HEAD_SCALE).astype(BF16)
        s = _bdot(kv_ref[pl.ds(start, blk), k_lanes], q2b)
        s = jnp.where(key_in <= qry_in, s, -jnp.inf)
        m0 = jnp.max(s, axis=0, keepdims=True)
        p = jnp.exp(s - m0)
        l0 = jnp.sum(p, axis=0, keepdims=True)
        acc_sc[...] = lax.dot_general(kv_ref[pl.ds(start, blk), v_lanes], p.astype(BF16), _TN,
                                      preferred_element_type=F32)

        def body(n, carry):
            m, l = carry
            off = pl.multiple_of(n * blk, blk)
            s = _bdot(kv_ref[pl.ds(off, blk), k_lanes], q2b)
            s = jnp.where(sel_sc[pl.ds(n, 1), :] > 0.5, s, -jnp.inf)
            m_new = jnp.maximum(m, jnp.max(s, axis=0, keepdims=True))
            alpha = jnp.exp(m - m_new)
            p = jnp.exp(s - m_new)
            acc_sc[...] = alpha * acc_sc[...] + lax.dot_general(
                kv_ref[pl.ds(off, blk), v_lanes], p.astype(BF16), _TN, preferred_element_type=F32)
            return m_new, alpha * l + jnp.sum(p, axis=0, keepdims=True)

        _, l = lax.fori_loop(0, cur, body, (m0, l0))
        o = (acc_sc[...] / l).T.astype(o_ref.dtype)
        o_ref[:, base:base + HEAD_DIM] = o[:blk]
        o_ref[:, base + HEAD_DIM:base + 2 * HEAD_DIM] = o[blk:]


def _moba_prefill(qb, kvb_bf, means, n_b, seq):
    blk = MOBA_BLOCK
    nb = seq // blk
    return pl.pallas_call(
        _moba_prefill_kernel,
        grid=(n_b, nb),
        in_specs=[pl.BlockSpec((blk, MOBA_Q_W), lambda b, i: (b * nb + i, 0)),
                  pl.BlockSpec((seq, 2 * MOBA_KV_W), lambda b, i: (b, 0)),
                  pl.BlockSpec((1, nb, MOBA_KV_W), lambda b, i: (b, 0, 0))],
        out_specs=pl.BlockSpec((blk, MOBA_Q_W), lambda b, i: (b * nb + i, 0)),
        out_shape=jax.ShapeDtypeStruct((n_b * seq, MOBA_Q_W), BF16),
        scratch_shapes=[pltpu.VMEM((nb, 2 * blk), F32), pltpu.VMEM((HEAD_DIM, 2 * blk), F32)],
        compiler_params=_cparams("parallel", "parallel"), name="moba_prefill",
    )(qb, kvb_bf, means)


def _out_proj_kernel(n_in, x_ref, *refs):
    a_refs, w_refs, o_ref = refs[:n_in], refs[n_in:2 * n_in], refs[2 * n_in]
    y = x_ref[...]
    for a_ref, w_ref in zip(a_refs, w_refs):
        y = y + _bdot(a_ref[...].astype(BF16), w_ref[...])
    o_ref[...] = y


def _out_proj(x, acts, ws):
    rows = x.shape[0]
    tm = ROW_TILE
    row = lambda w: pl.BlockSpec((tm, w), lambda i: (i, 0))
    return pl.pallas_call(
        functools.partial(_out_proj_kernel, len(acts)),
        grid=(rows // tm,),
        in_specs=[row(D_MODEL)] + [row(a.shape[1]) for a in acts] + [_const_spec(w.shape) for w in ws],
        out_specs=row(D_MODEL), out_shape=jax.ShapeDtypeStruct((rows, D_MODEL), F32),
        compiler_params=_cparams("parallel"), name="out_proj",
    )(x, *acts, *ws)


def _dil_merge_proj_kernel(x_ref, o0, o1, o2, l0, l1, l2, w_ref, out_ref):
    lses = [l0[...], l1[...], l2[...]]
    outs = [o0[...], o1[...], o2[...]]
    mx = jnp.maximum(jnp.maximum(lses[0], lses[1]), lses[2])
    ws = [jnp.exp(l - mx) for l in lses]
    tot = ws[0] + ws[1] + ws[2]
    o = (ws[0] * outs[0] + ws[1] * outs[1] + ws[2] * outs[2]) / tot
    out_ref[...] = x_ref[...] + _bdot(o.astype(BF16), w_ref[...])


def _dil_merge_proj(x, outs, lses, w):
    rows = x.shape[0]
    tm = ROW_TILE
    row = lambda wd: pl.BlockSpec((tm, wd), lambda i: (i, 0))
    return pl.pallas_call(
        _dil_merge_proj_kernel,
        grid=(rows // tm,),
        in_specs=[row(D_MODEL)] + [row(DIL_W)] * 6 + [_const_spec(w.shape)],
        out_specs=row(D_MODEL), out_shape=jax.ShapeDtypeStruct((rows, D_MODEL), F32),
        compiler_params=_cparams("parallel"), name="dil_merge_proj",
    )(x, *outs, *lses, w)


def _c_kernel(x_ref, gm_ref, w_ref, t64_ref, q_ref, kv_ref):
    h = _rms(x_ref[...], gm_ref[...]).astype(BF16)
    z = _bdot(h, w_ref[...])
    t64 = t64_ref[...]
    q_w = N_DIL * DIL_W
    q_ref[...] = _rope64(z[:, :q_w], t64)
    for g in range(N_DIL):
        base = q_w + 2 * DIL_W * g
        kv_ref[:, 2 * DIL_W * g:2 * DIL_W * g + DIL_W] = _rope64(z[:, base:base + DIL_W], t64)
        kv_ref[:, 2 * DIL_W * g + DIL_W:2 * DIL_W * (g + 1)] = z[:, base + DIL_W:base + 2 * DIL_W]


def _c_project(x, g_mix, w, t64):
    rows = x.shape[0]
    tm = ROW_TILE
    n_tab = t64.shape[0] // tm
    row = lambda wd: pl.BlockSpec((tm, wd), lambda i: (i, 0))
    return pl.pallas_call(
        _c_kernel,
        grid=(rows // tm,),
        in_specs=[row(D_MODEL), _const_spec((1, D_MODEL)), _const_spec(w.shape),
                  pl.BlockSpec((tm, 384), lambda i: (i % n_tab, 0))],
        out_specs=[row(N_DIL * DIL_W), row(2 * N_DIL * DIL_W)],
        out_shape=[jax.ShapeDtypeStruct((rows, N_DIL * DIL_W), F32),
                   jax.ShapeDtypeStruct((rows, 2 * N_DIL * DIL_W), F32)],
        compiler_params=_cparams("parallel"), name="c_project",
    )(x, g_mix, w, t64)


def _dil_prefill_kernel(q_ref, kp_ref, kc_ref, vp_ref, vc_ref, o_ref, lse_ref):
    tq = DIL_TQ
    i = pl.program_id(2)
    q = (q_ref[0] * HEAD_SCALE).astype(BF16)
    k2 = jnp.concatenate([kp_ref[0], kc_ref[0]], axis=0).astype(BF16)
    v2 = jnp.concatenate([vp_ref[0], vc_ref[0]], axis=0).astype(BF16)
    row = lax.broadcasted_iota(jnp.int32, (tq, 2 * tq), 0)
    col = lax.broadcasted_iota(jnp.int32, (tq, 2 * tq), 1)
    dist = row + tq - col
    first_col = jnp.where(i > 0, 0, tq)
    valid = (dist >= 0) & (dist <= DIL_KEYS) & (col >= first_col)
    outs, lses = [], []
    for hh in range(DIL_HEADS):
        lanes = slice(HEAD_DIM * hh, HEAD_DIM * (hh + 1))
        s = lax.dot_general(q[:, lanes], k2[:, lanes], _NT, preferred_element_type=F32)
        s = jnp.where(valid, s, -jnp.inf)
        m = jnp.max(s, axis=1, keepdims=True)
        p = jnp.exp(s - m)
        l = jnp.sum(p, axis=1, keepdims=True)
        outs.append(_bdot(p.astype(BF16), v2[:, lanes]) / l)
        lses.append(jnp.broadcast_to(m + jnp.log(l), (tq, HEAD_DIM)))
    o_ref[0] = jnp.concatenate(outs, axis=-1)
    lse_ref[0] = jnp.concatenate(lses, axis=-1)


def _dil_prefill(q, kv, g, n_b, seq):
    dil = DIL_CFG[g][1]
    tq = DIL_TQ
    s_rows = seq // dil
    nt = s_rows // tq
    qv = q.reshape(n_b, s_rows, dil * N_DIL * DIL_W)
    kvv = kv.reshape(n_b, s_rows, dil * 2 * N_DIL * DIL_W)
    blk = (1, tq, DIL_W)
    kcol = lambda r: r * 2 * N_DIL + 2 * g
    out, lse = pl.pallas_call(
        _dil_prefill_kernel,
        grid=(n_b, dil, nt),
        in_specs=[pl.BlockSpec(blk, lambda b, r, i: (b, i, r * N_DIL + g)),
                  pl.BlockSpec(blk, lambda b, r, i: (b, jnp.maximum(i - 1, 0), kcol(r))),
                  pl.BlockSpec(blk, lambda b, r, i: (b, i, kcol(r))),
                  pl.BlockSpec(blk, lambda b, r, i: (b, jnp.maximum(i - 1, 0), kcol(r) + 1)),
                  pl.BlockSpec(blk, lambda b, r, i: (b, i, kcol(r) + 1))],
        out_specs=[pl.BlockSpec(blk, lambda b, r, i: (b, i, r))] * 2,
        out_shape=[jax.ShapeDtypeStruct((n_b, s_rows, dil * DIL_W), F32)] * 2,
        compiler_params=_cparams("parallel", "parallel", "parallel"), name=f"dil_prefill_g{g}",
    )(qv, kvv, kvv, kvv, kvv)
    return out.reshape(n_b * seq, DIL_W), lse.reshape(n_b * seq, DIL_W)


def _ffn_kernel(mode, steps_per_seq, x_ref, g_ref, wup_ref, wc_ref, bc_ref, wd_ref, *refs):
    tm = x_ref.shape[0]
    if mode == "long":
        o_ref, tail_ref, ext = refs
    else:
        e1_ref, e2_ref, o_ref, tail_ref, ext = refs
    x = x_ref[...]
    h = _rms(x, g_ref[...]).astype(BF16)
    gu = _bdot(h, wup_ref[...])
    gate, up = gu[:, :D_FF], gu[:, D_FF:]
    if mode == "long":
        @pl.when(pl.program_id(0) % steps_per_seq == 0)
        def _():
            ext[0:8, :] = jnp.zeros((8, D_FF), F32)
    else:
        ext[0:8, :] = jnp.zeros((8, D_FF), F32)
    ext[8:8 + tm, :] = gate
    g1 = ext[7:7 + tm, :]
    g2 = ext[6:6 + tm, :]
    if mode == "long":
        ext[0:8, :] = gate[tm - 8:, :]
        tail_ref[0] = gate[tm - 8:, :]
    else:
        t = lax.broadcasted_iota(jnp.int32, (tm, D_FF), 0) % 8
        g1 = jnp.where(t < 1, e1_ref[...], g1)
        g2 = jnp.where(t < 2, e2_ref[...], g2)
        tail_ref[...] = gate
    wc = wc_ref[...]
    conv = bc_ref[...] + wc[0:1] * g2 + wc[1:2] * g1 + wc[2:3] * gate
    hid = (conv * jax.nn.sigmoid(conv) * up).astype(BF16)
    o_ref[...] = x + _bdot(hid, wd_ref[...])


def _conv_ffn(x, g, wup, wc, bc, wd, seq, hist=None):
    rows = x.shape[0]
    tm = ROW_TILE
    n_seq = rows // seq
    row = lambda w: pl.BlockSpec((tm, w), lambda i: (i, 0))
    consts = [_const_spec((1, D_MODEL)), _const_spec(wup.shape), _const_spec(wc.shape),
              _const_spec((1, D_FF)), _const_spec(wd.shape)]
    if hist is None:
        steps = seq // tm
        y, tail = pl.pallas_call(
            functools.partial(_ffn_kernel, "long", steps),
            grid=(rows // tm,),
            in_specs=[row(D_MODEL)] + consts,
            out_specs=[row(D_MODEL), pl.BlockSpec((1, 8, D_FF), lambda i: (i // steps, 0, 0))],
            out_shape=[jax.ShapeDtypeStruct((rows, D_MODEL), F32), jax.ShapeDtypeStruct((n_seq, 8, D_FF), F32)],
            scratch_shapes=[pltpu.VMEM((tm + 8, D_FF), F32)],
            compiler_params=_cparams("arbitrary"), name="conv_ffn_prompt",
        )(x, g, wup, wc, bc, wd)
        return y, tail[:, 8 - (CONV_W - 1):]
    assert seq == 8 and hist.shape == (n_seq, CONV_W - 1, D_FF)
    zeros = jnp.zeros((n_seq, 8 - (CONV_W - 1), D_FF), F32)
    e2 = jnp.concatenate([hist, zeros], axis=1).reshape(rows, D_FF)
    e1 = jnp.concatenate([hist[:, 1:], zeros, zeros[:, :1]], axis=1).reshape(rows, D_FF)
    y, gate = pl.pallas_call(
        functools.partial(_ffn_kernel, "short", 1),
        grid=(rows // tm,),
        in_specs=[row(D_MODEL)] + consts + [row(D_FF), row(D_FF)],
        out_specs=[row(D_MODEL), row(D_FF)],
        out_shape=[jax.ShapeDtypeStruct((rows, D_MODEL), F32), jax.ShapeDtypeStruct((rows, D_FF), F32)],
        scratch_shapes=[pltpu.VMEM((tm + 8, D_FF), F32)],
        compiler_params=_cparams("arbitrary"), name="conv_ffn_sample",
    )(x, g, wup, wc, bc, wd, e1, e2)
    return y, gate.reshape(n_seq, seq, D_FF)[:, seq - (CONV_W - 1):]


def _mla_decode_kernel(pt_ref, q_ref, new_ref, wuv_ref, *refs):
    pps = PAGES_PER_STEP
    pages, (o_ref, kbuf, m_sc, l_sc, acc_sc) = refs[:pps], refs[pps:]
    c = pl.program_id(1)
    rows = MLA_HEADS * 8
    q = q_ref[...].reshape(rows, LAT_W).astype(BF16)

    @pl.when(c == 0)
    def _():
        k = jnp.concatenate([new_ref[...], jnp.zeros((PAGE_SIZE - 8, LAT_W), F32)], axis=0).astype(BF16)
        s = lax.dot_general(q, k, _NT, preferred_element_type=F32)
        t = lax.broadcasted_iota(jnp.int32, s.shape, 0) % 8
        col = lax.broadcasted_iota(jnp.int32, s.shape, 1)
        s = jnp.where(col <= t, s, -jnp.inf)
        m = jnp.max(s, axis=1, keepdims=True)
        p = jnp.exp(s - m)
        m_sc[...] = m
        l_sc[...] = jnp.sum(p, axis=1, keepdims=True)
        acc_sc[...] = _bdot(p.astype(BF16), k[:, :KV_LORA])

    for j in range(pps):
        kbuf[:, PAGE_SIZE * j:PAGE_SIZE * (j + 1)] = pages[j][0].astype(BF16)
    k_t = kbuf[...]
    s = _bdot(q, k_t)
    m, l, acc = _online_update(s, k_t[:KV_LORA], m_sc[...], l_sc[...], acc_sc[...], v_transposed=True)
    m_sc[...] = m
    l_sc[...] = l
    acc_sc[...] = acc

    @pl.when(c == pl.num_programs(1) - 1)
    def _():
        o_lat = (acc_sc[...] / l_sc[...]).astype(BF16)
        o_ref[...] = jnp.concatenate(
            [_bdot(o_lat[8 * hh:8 * (hh + 1)], wuv_ref[hh]) for hh in range(MLA_HEADS)], axis=-1)


def _mla_decode(q, lat_new, wuv, pool_t, page_table):
    n_b, n_pages = page_table.shape
    pps = PAGES_PER_STEP
    steps = n_pages // pps
    rows = MLA_HEADS * 8
    page = lambda j: pl.BlockSpec((1, LAT_W, PAGE_SIZE), lambda b, c, pt: (pt[b, c * pps + j], 0, 0))
    return pl.pallas_call(
        _mla_decode_kernel,
        grid_spec=pltpu.PrefetchScalarGridSpec(
            num_scalar_prefetch=1, grid=(n_b, steps),
            in_specs=[pl.BlockSpec((MLA_HEADS, 8, LAT_W), lambda b, c, pt: (0, b, 0)),
                      pl.BlockSpec((8, LAT_W), lambda b, c, pt: (b, 0)),
                      _const_spec(wuv.shape)] + [page(j) for j in range(pps)],
            out_specs=pl.BlockSpec((8, MLA_HEADS * MLA_V), lambda b, c, pt: (b, 0)),
            scratch_shapes=[pltpu.VMEM((LAT_W, pps * PAGE_SIZE), BF16), pltpu.VMEM((rows, 1), F32),
                            pltpu.VMEM((rows, 1), F32), pltpu.VMEM((rows, KV_LORA), F32)]),
        out_shape=jax.ShapeDtypeStruct((n_b * 8, MLA_HEADS * MLA_V), F32),
        compiler_params=_cparams("parallel", "arbitrary"), name="mla_decode",
    )(page_table, q, lat_new, wuv, *([pool_t] * pps))


def _moba_decode_kernel(pt_ref, q_ref, new_ref, *refs):
    pps = PAGES_PER_STEP
    pages, (o_ref, kvbuf, means_sc) = refs[:pps], refs[pps:]
    c = pl.program_id(1)
    n_chunks, _, chunk = kvbuf.shape
    blocks_per_step = chunk // MOBA_BLOCK
    pages_per_block = MOBA_BLOCK // PAGE_SIZE
    nb = n_chunks * blocks_per_step

    @pl.when(c == 0)
    def _():
        means_sc[...] = jnp.zeros(means_sc.shape, F32)

    lane = lax.broadcasted_iota(jnp.int32, means_sc.shape, 1)
    means_t = means_sc[...]
    for jb in range(blocks_per_step):
        ksum = None
        for jp in range(pages_per_block):
            j = pages_per_block * jb + jp
            pg = pages[j][0]
            kvbuf[c, :, PAGE_SIZE * j:PAGE_SIZE * (j + 1)] = pg.astype(BF16)
            ksum = pg[:MOBA_KV_W] if ksum is None else ksum + pg[:MOBA_KV_W]
        mean_col = jnp.sum(ksum, axis=1, keepdims=True) * (1.0 / MOBA_BLOCK)
        means_t = jnp.where(lane == c * blocks_per_step + jb, mean_col, means_t)
    means_sc[...] = means_t

    @pl.when(c == pl.num_programs(1) - 1)
    def _():
        q = q_ref[...]
        grp = MOBA_HEADS // MOBA_KV_HEADS
        pieces = []
        for hh in range(MOBA_HEADS):
            kvh = hh // grp
            parts = []
            if kvh > 0:
                parts.append(jnp.zeros((8, HEAD_DIM * kvh), F32))
            parts.append(q[:, HEAD_DIM * hh:HEAD_DIM * (hh + 1)])
            if kvh < MOBA_KV_HEADS - 1:
                parts.append(jnp.zeros((8, HEAD_DIM * (MOBA_KV_HEADS - 1 - kvh)), F32))
            pieces.append(jnp.concatenate(parts, axis=-1))
        qbd = jnp.concatenate(pieces, axis=0)
        gate = jnp.dot(qbd, means_t, precision=lax.Precision.HIGHEST, preferred_element_type=F32)
        sel = _top_blocks(gate, lax.broadcasted_iota(jnp.int32, gate.shape, 1) < nb, 1)
        qb = (qbd * HEAD_SCALE).astype(BF16)
        new = jnp.concatenate([new_ref[...], jnp.zeros((PAGE_SIZE - 8, 2 * MOBA_KV_W), F32)], axis=0).astype(BF16)
        s = lax.dot_general(qb, new[:, :MOBA_KV_W], _NT, preferred_element_type=F32)
        t = lax.broadcasted_iota(jnp.int32, s.shape, 0) % 8
        col = lax.broadcasted_iota(jnp.int32, s.shape, 1)
        s = jnp.where(col <= t, s, -jnp.inf)
        m = jnp.max(s, axis=1, keepdims=True)
        p = jnp.exp(s - m)
        l = jnp.sum(p, axis=1, keepdims=True)
        acc = _bdot(p.astype(BF16), new[:, MOBA_KV_W:])
        selb = sel.astype(BF16)
        n_lanes = sel.shape[1]
        for ch in range(n_chunks):
            kv_t = kvbuf[ch]
            s = _bdot(qb, kv_t[:MOBA_KV_W])
            blk_of_col = lax.broadcasted_iota(jnp.int32, (n_lanes, chunk), 1) // MOBA_BLOCK + ch * blocks_per_step
            expand = jnp.where(lax.broadcasted_iota(jnp.int32, (n_lanes, chunk), 0) == blk_of_col, 1.0, 0.0)
            s = jnp.where(_bdot(selb, expand.astype(BF16)) > 0.5, s, -jnp.inf)
            m, l, acc = _online_update(s, kv_t[MOBA_KV_W:], m, l, acc, v_transposed=True)
        o = acc / l
        o_ref[...] = jnp.concatenate(
            [o[8 * hh:8 * (hh + 1), HEAD_DIM * (hh // grp):HEAD_DIM * (hh // grp + 1)] for hh in range(MOBA_HEADS)],
            axis=-1)


def _moba_decode(qb, kv_new, pool_t, page_table):
    n_b, n_pages = page_table.shape
    pps = PAGES_PER_STEP
    steps = n_pages // pps
    assert n_pages * PAGE_SIZE // MOBA_BLOCK <= 128
    page = lambda j: pl.BlockSpec((1, 2 * MOBA_KV_W, PAGE_SIZE), lambda b, c, pt: (pt[b, c * pps + j], 0, 0))
    return pl.pallas_call(
        _moba_decode_kernel,
        grid_spec=pltpu.PrefetchScalarGridSpec(
            num_scalar_prefetch=1, grid=(n_b, steps),
            in_specs=[pl.BlockSpec((8, MOBA_Q_W), lambda b, c, pt: (b, 0)),
                      pl.BlockSpec((8, 2 * MOBA_KV_W), lambda b, c, pt: (b, 0))] + [page(j) for j in range(pps)],
            out_specs=pl.BlockSpec((8, MOBA_Q_W), lambda b, c, pt: (b, 0)),
            scratch_shapes=[pltpu.VMEM((steps, 2 * MOBA_KV_W, pps * PAGE_SIZE), BF16),
                            pltpu.VMEM((MOBA_KV_W, 128), F32)]),
        out_shape=jax.ShapeDtypeStruct((n_b * 8, MOBA_Q_W), F32),
        compiler_params=_cparams("parallel", "arbitrary"), name="moba_decode",
    )(page_table, qb, kv_new, *([pool_t] * pps))


def _dil_decode_kernel(q_ref, new_ref, st_ref, o_ref, win_ref, p_sc, pn_sc, l_sc):
    c = pl.program_id(1)
    rows = DIL_HEADS * 8
    starts = [sum(w for w, _ in DIL_CFG[:g]) for g in range(N_DIL)]
    ends = [starts[g] + DIL_CFG[g][0] for g in range(N_DIL)]
    new = new_ref[...]
    x = st_ref[0]
    pad_rows = jnp.zeros((128 - 8 * N_DIL, DIL_W), F32)

    rolled = pltpu.roll(x, WIN_ROWS - 8, 1)
    win_ref[0] = rolled
    lane = lax.broadcasted_iota(jnp.int32, (DIL_W, 128), 1)
    for g in range(N_DIL):
        new_g = jnp.where(c == 0, new[:, 2 * DIL_W * g:2 * DIL_W * g + DIL_W],
                          new[:, 2 * DIL_W * g + DIL_W:2 * DIL_W * (g + 1)])
        tile_t = jnp.concatenate([jnp.zeros((128 - 8, DIL_W), F32), new_g], axis=0).T
        win_ref[0, :, ends[g] - 128:ends[g]] = jnp.where(lane >= 128 - 8, tile_t, rolled[:, ends[g] - 128:ends[g]])

    @pl.when(c == 0)
    def _():
        q = q_ref[...] * HEAD_SCALE
        r_head = lax.broadcasted_iota(jnp.int32, (rows, DIL_W), 0) // 8
        l_head = lax.broadcasted_iota(jnp.int32, (rows, DIL_W), 1) // HEAD_DIM
        q_all = jnp.concatenate(
            [jnp.where(r_head == l_head, jnp.tile(q[:, DIL_W * g:DIL_W * (g + 1)], (DIL_HEADS, 1)), 0.0)
             for g in range(N_DIL)], axis=0).astype(BF16)

        def pick_group(s_all, grp_of_col):
            return jnp.where(grp_of_col == 0, s_all[0:rows],
                             jnp.where(grp_of_col == 1, s_all[rows:2 * rows], s_all[2 * rows:3 * rows]))

        kn = jnp.concatenate([new[:, 2 * DIL_W * g:2 * DIL_W * g + DIL_W] for g in range(N_DIL)] + [pad_rows],
                             axis=0).astype(BF16)
        col = lax.broadcasted_iota(jnp.int32, (rows, 128), 1)
        t = lax.broadcasted_iota(jnp.int32, (rows, 128), 0) % 8
        grp = col // 8
        dist = t - col % 8
        dmask = jnp.where(grp == 0, DIL_CFG[0][1] - 1, jnp.where(grp == 1, DIL_CFG[1][1] - 1, DIL_CFG[2][1] - 1))
        ok = (grp < N_DIL) & (dist >= 0) & ((dist & dmask) == 0)
        s_new = jnp.where(ok, pick_group(lax.dot_general(q_all, kn, _NT, preferred_element_type=F32), grp), -jnp.inf)

        e_glob = lax.broadcasted_iota(jnp.int32, (rows, WIN_ROWS), 1)
        t = lax.broadcasted_iota(jnp.int32, (rows, WIN_ROWS), 0) % 8
        grp = (e_glob >= starts[1]).astype(jnp.int32) + (e_glob >= starts[2]).astype(jnp.int32)
        pick = lambda vals: jnp.where(grp == 0, vals[0], jnp.where(grp == 1, vals[1], vals[2]))
        win = pick([w for w, _ in DIL_CFG])
        dist = win + t - (e_glob - pick(starts))
        ok = (dist <= win) & ((dist & pick([d - 1 for _, d in DIL_CFG])) == 0)
        s_old = jnp.where(ok, pick_group(_bdot(q_all, x.astype(BF16)), grp), -jnp.inf)

        m = jnp.maximum(jnp.max(s_new, axis=1, keepdims=True), jnp.max(s_old, axis=1, keepdims=True))
        p_new = jnp.exp(s_new - m)
        p_old = jnp.exp(s_old - m)
        l_sc[...] = jnp.sum(p_new, axis=1, keepdims=True) + jnp.sum(p_old, axis=1, keepdims=True)
        pn_sc[...] = p_new.astype(BF16)
        p_sc[...] = p_old.astype(BF16)

    @pl.when(c == 1)
    def _():
        vn = jnp.concatenate([new[:, 2 * DIL_W * g + DIL_W:2 * DIL_W * (g + 1)] for g in range(N_DIL)] + [pad_rows],
                             axis=0).astype(BF16)
        acc = (lax.dot_general(p_sc[...], x.astype(BF16), _NT, preferred_element_type=F32)
               + _bdot(pn_sc[...], vn))
        o = acc / l_sc[...]
        o_ref[...] = jnp.concatenate(
            [o[8 * hh:8 * (hh + 1), HEAD_DIM * hh:HEAD_DIM * (hh + 1)] for hh in range(DIL_HEADS)], axis=-1)


def _dil_decode(q, kv_new, state):
    n_b = state.shape[0]
    st_t = jnp.transpose(state, (0, 2, 3, 4, 1)).reshape(n_b, 2 * DIL_W, WIN_ROWS)
    rows = DIL_HEADS * 8
    half = pl.BlockSpec((1, DIL_W, WIN_ROWS), lambda b, c: (b, c, 0))
    o, win_t = pl.pallas_call(
        _dil_decode_kernel,
        grid=(n_b, 2),
        in_specs=[pl.BlockSpec((8, N_DIL * DIL_W), lambda b, c: (b, 0)),
                  pl.BlockSpec((8, 2 * N_DIL * DIL_W), lambda b, c: (b, 0)), half],
        out_specs=[pl.BlockSpec((8, DIL_W), lambda b, c: (b, 0)), half],
        out_shape=[jax.ShapeDtypeStruct((n_b * 8, DIL_W), F32),
                   jax.ShapeDtypeStruct((n_b, 2 * DIL_W, WIN_ROWS), F32)],
        scratch_shapes=[pltpu.VMEM((rows, WIN_ROWS), BF16), pltpu.VMEM((rows, 128), BF16), pltpu.VMEM((rows, 1), F32)],
        compiler_params=_cparams("parallel", "arbitrary"), name="dil_decode",
    )(q, kv_new, st_t)
    win = jnp.transpose(win_t.reshape(n_b, 2, DIL_HEADS, HEAD_DIM, WIN_ROWS), (0, 4, 1, 2, 3))
    return o, win


def _final_norm_kernel(x_ref, g_ref, o_ref):
    o_ref[...] = _rms(x_ref[...], g_ref[...])


def _final_norm(x, g):
    rows = x.shape[0]
    tm = ROW_TILE
    row = pl.BlockSpec((tm, D_MODEL), lambda i: (i, 0))
    return pl.pallas_call(
        _final_norm_kernel, grid=(rows // tm,),
        in_specs=[row, _const_spec((1, D_MODEL))], out_specs=row,
        out_shape=jax.ShapeDtypeStruct((rows, D_MODEL), F32),
        compiler_params=_cparams("parallel"), name="final_norm",
    )(x, g)


def _rope_tables(pos):
    n = pos.shape[0]
    posf = pos.astype(F32)[:, None]
    half = ROT_DIM // 2
    ang = posf * (ROPE_THETA ** (-jnp.arange(half, dtype=F32) / half))[None, :]
    c, s = jnp.cos(ang), jnp.sin(ang)
    rest = HEAD_DIM - ROT_DIM
    cos64 = jnp.concatenate([c, c, jnp.ones((n, rest), F32)], axis=1)
    sin_hi = jnp.concatenate([-s, jnp.zeros((n, half + rest), F32)], axis=1)
    sin_lo = jnp.concatenate([jnp.zeros((n, half), F32), s, jnp.zeros((n, rest), F32)], axis=1)
    t64 = jnp.concatenate([jnp.tile(cos64, (1, 2)), jnp.tile(sin_hi, (1, 2)), jnp.tile(sin_lo, (1, 2))], axis=1)
    half = MLA_ROPE // 2
    ang = posf * (ROPE_THETA ** (-jnp.arange(half, dtype=F32) / half))[None, :]
    c, s = jnp.cos(ang), jnp.sin(ang)
    t32 = jnp.concatenate([jnp.tile(jnp.concatenate([c, c], axis=1), (1, 4)),
                           jnp.tile(jnp.concatenate([s, s], axis=1), (1, 4))], axis=1)
    return t64, t32


def _rot_cols(w, half):
    return jnp.concatenate([-w[..., half:], w[..., :half]], axis=-1)


def _even_weights(w_in, g_q, g_kv, w_uq, w_uk, w_uv, w_o):
    o1 = Q_LORA
    o2 = o1 + KV_LORA
    o3 = o2 + MLA_ROPE
    w_kr = w_in[:, o2:o3]
    win = jnp.concatenate([w_in[:, o3:], w_in[:, :o1], w_in[:, o1:o2], w_kr, _rot_cols(w_kr, MLA_ROPE // 2)], axis=1)
    w_nope = w_uq[:, :, :MLA_NOPE].reshape(Q_LORA, MLA_HEADS * MLA_NOPE)
    w_rope = w_uq[:, :, MLA_NOPE:]
    wq2 = jnp.concatenate([w_nope, w_rope.reshape(Q_LORA, MLA_HEADS * MLA_ROPE),
                           _rot_cols(w_rope, MLA_ROPE // 2).reshape(Q_LORA, MLA_HEADS * MLA_ROPE)], axis=1)
    n_a = MLA_HEADS * MLA_V
    return dict(win=win.astype(BF16), gq=g_q[None, :], gkv=g_kv[None, :], wq2=wq2.astype(BF16),
                wukt=jnp.transpose(w_uk, (0, 2, 1)).astype(BF16), wuv=w_uv.astype(BF16),
                wo_a=w_o[:n_a].astype(BF16), wo_b=w_o[n_a:].astype(BF16))


def _odd_weights(w_qkv):
    q_w = N_DIL * DIL_W
    cols = [w_qkv[:, :q_w]]
    for g in range(N_DIL):
        cols.append(w_qkv[:, q_w + DIL_W * g:q_w + DIL_W * (g + 1)])
        cols.append(w_qkv[:, 2 * q_w + DIL_W * g:2 * q_w + DIL_W * (g + 1)])
    return jnp.concatenate(cols, axis=1).astype(BF16)


def kernel(x_prompt, x_sample, cache_mla_l0, cache_moba_l0, state_win_l1, cache_mla_l2, cache_moba_l2, state_win_l3, state_ffn_conv, page_table, g_mix, g_ffn, g_final, w_in_ab, g_q_lat, g_kv_lat, w_uq, w_uk, w_uv, w_o_ab, w_qkv_c, w_o_c, w_up, w_conv, b_conv, w_down):
    n_p, s_p = x_prompt.shape[:2]
    n_s, s_s = x_sample.shape[:2]
    n_pages = page_table.shape[1]
    past_len = n_pages * PAGE_SIZE
    depth = g_mix.shape[0]
    max_win = max(w for w, _ in DIL_CFG)
    assert s_s == 8 and s_p % max_win == 0 and past_len % (PAGES_PER_STEP * PAGE_SIZE) == 0
    assert past_len >= max_win and (n_s * s_s) % ROW_TILE == 0
    assert s_p // MOBA_BLOCK >= MOBA_TOPK and past_len // MOBA_BLOCK >= MOBA_TOPK

    t64_p, t32_p = _rope_tables(jnp.arange(s_p))
    t64_s, t32_s = _rope_tables(jnp.tile(past_len + jnp.arange(s_s), n_s))
    mla_caches = (cache_mla_l0, cache_mla_l2)
    moba_caches = (cache_moba_l0, cache_moba_l2)
    win_states = (state_win_l1, state_win_l3)

    xp = x_prompt.reshape(n_p * s_p, D_MODEL)
    xs = x_sample.reshape(n_s * s_s, D_MODEL)
    lat_p, lat_s, kvb_p, kvb_s, win_p, win_s, conv_p, conv_s = [], [], [], [], [], [], [], []
    for layer in range(depth):
        i = layer // 2
        gm = g_mix[layer][None, :]
        if layer % 2 == 0:
            wts = _even_weights(w_in_ab[i], g_q_lat[i], g_kv_lat[i], w_uq[i], w_uk[i], w_uv[i], w_o_ab[i])
            q, lat, lat_bf, qb, kv, kv_bf, means = _ab_project(xp, gm, wts, t64_p, t32_p, BF16, True)
            oa = _mla_prefill(q, lat_bf, wts['wuv'], n_p, s_p)
            ob = _moba_prefill(qb, kv_bf, means.reshape(n_p, s_p // MOBA_BLOCK, MOBA_KV_W), n_p, s_p)
            xp = _out_proj(xp, [oa, ob], [wts['wo_a'], wts['wo_b']])
            lat_p.append(lat.reshape(n_p, s_p, LAT_W))
            kvb_p.append(kv.reshape(n_p, s_p, 2, MOBA_KV_HEADS, HEAD_DIM))
            q2, lat2, _, qb2, kv2, _ = _ab_project(xs, gm, wts, t64_s, t32_s, F32, False)
            mla_pool_t = jnp.transpose(mla_caches[i], (0, 2, 1))
            oa2 = _mla_decode(q2, lat2, wts['wuv'], mla_pool_t, page_table)
            moba_pool_t = jnp.transpose(moba_caches[i], (0, 2, 3, 4, 1)).reshape(-1, 2 * MOBA_KV_W, PAGE_SIZE)
            ob2 = _moba_decode(qb2, kv2, moba_pool_t, page_table)
            xs = _out_proj(xs, [oa2, ob2], [wts['wo_a'], wts['wo_b']])
            lat_s.append(lat2.reshape(n_s, s_s, LAT_W))
            kvb_s.append(kv2.reshape(n_s, s_s, 2, MOBA_KV_HEADS, HEAD_DIM))
        else:
            wqkv = _odd_weights(w_qkv_c[i])
            wo = w_o_c[i].astype(BF16)
            qd, kvd = _c_project(xp, gm, wqkv, t64_p)
            parts = [_dil_prefill(qd, kvd, g, n_p, s_p) for g in range(N_DIL)]
            xp = _dil_merge_proj(xp, [o for o, _ in parts], [l for _, l in parts], wo)
            kv5 = kvd.reshape(n_p, s_p, N_DIL, 2, DIL_HEADS, HEAD_DIM)
            win_p.append(jnp.concatenate([kv5[:, s_p - w:, g] for g, (w, _) in enumerate(DIL_CFG)], axis=1))
            qd2, kvd2 = _c_project(xs, gm, wqkv, t64_s)
            oc2, win2 = _dil_decode(qd2, kvd2, win_states[i])
            xs = _out_proj(xs, [oc2], [wo])
            win_s.append(win2)
        wup = w_up[layer].astype(BF16)
        wd = w_down[layer].astype(BF16)
        gf = g_ffn[layer][None, :]
        bc = b_conv[layer][None, :]
        xp, cp = _conv_ffn(xp, gf, wup, w_conv[layer], bc, wd, s_p)
        xs, cs = _conv_ffn(xs, gf, wup, w_conv[layer], bc, wd, s_s, hist=state_ffn_conv[layer])
        conv_p.append(cp)
        conv_s.append(cs)
    gfin = g_final[None, :]
    y_prompt = _final_norm(xp, gfin).reshape(n_p, s_p, D_MODEL)
    y_sample = _final_norm(xs, gfin).reshape(n_s, s_s, D_MODEL)
    return (y_prompt, y_sample,
            lat_p[0], lat_s[0], kvb_p[0], kvb_s[0], win_p[0], win_s[0],
            lat_p[1], lat_s[1], kvb_p[1], kvb_s[1], win_p[1], win_s[1],
            jnp.stack(conv_p), jnp.stack(conv_s))
```

```python
import functools

import jax
import jax.numpy as jnp
from jax import lax
from jax.experimental import pallas as pl
from jax.experimental.pallas import tpu as pltpu

F32 = jnp.float32
BF16 = jnp.bfloat16

D_MODEL = 1024
PAGE_SIZE = 128
HEAD_DIM = 64
ROT_DIM = HEAD_DIM // 4
ROPE_THETA = 500000.0
NORM_EPS = 1e-6
HEAD_SCALE = HEAD_DIM ** -0.5

MLA_HEADS = 8
MLA_NOPE = 64
MLA_ROPE = 32
MLA_V = 64
Q_LORA = 384
KV_LORA = 256
LAT_W = KV_LORA + MLA_ROPE
MLA_SCALE = (MLA_NOPE + MLA_ROPE) ** -0.5

MOBA_HEADS = 8
MOBA_KV_HEADS = 4
MOBA_BLOCK = 256
MOBA_TOPK = 3
MOBA_Q_W = MOBA_HEADS * HEAD_DIM
MOBA_KV_W = MOBA_KV_HEADS * HEAD_DIM

DIL_CFG = ((128, 1), (512, 4), (2048, 16))
N_DIL = len(DIL_CFG)
DIL_HEADS = 8
DIL_W = DIL_HEADS * HEAD_DIM
DIL_KEYS = 128
WIN_ROWS = sum(w for w, _ in DIL_CFG)

D_FF = 2816
CONV_W = 3

VMEM_LIMIT = 56 * 1024 * 1024

ROW_TILE = 256
MLA_TQ = 128
MLA_TK = 512
DIL_TQ = 128
PAGES_PER_STEP = 16
MLA_PAGES_PER_STEP = 32

_NT = (((1,), (1,)), ((), ()))
_TN = (((0,), (0,)), ((), ()))


def _cparams(*sem):
    return pltpu.CompilerParams(dimension_semantics=sem, vmem_limit_bytes=VMEM_LIMIT)


def _const_spec(shape):
    nd = len(shape)
    return pl.BlockSpec(shape, lambda *_: (0,) * nd, pipeline_mode=pl.Buffered(1))


def _rms(x, g):
    return x * lax.rsqrt(jnp.mean(x * x, axis=-1, keepdims=True) + NORM_EPS) * g


def _rope64(x, tab):
    w = x.shape[1]
    n = w // 128
    c = jnp.tile(tab[:, 0:128], (1, n))
    s1 = jnp.tile(tab[:, 128:256], (1, n))
    s2 = jnp.tile(tab[:, 256:384], (1, n))
    half = ROT_DIM // 2
    x_hi = pltpu.roll(x, w - half, 1)
    x_lo = pltpu.roll(x, half, 1)
    return x * c + x_hi * s1 + x_lo * s2


def _bdot(a, b):
    return jnp.dot(a, b, preferred_element_type=F32)


def _online_update(s, v, m, l, acc, v_transposed=False):
    m_new = jnp.maximum(m, jnp.max(s, axis=1, keepdims=True))
    alpha = jnp.exp(m - m_new)
    p = jnp.exp(s - m_new)
    l_new = alpha * l + jnp.sum(p, axis=1, keepdims=True)
    if v_transposed:
        pv = lax.dot_general(p.astype(BF16), v, _NT, preferred_element_type=F32)
    else:
        pv = _bdot(p.astype(BF16), v)
    return m_new, l_new, alpha * acc + pv


def _ab_kernel(with_means, x_ref, gm_ref, win_ref, gq_ref, gkv_ref, wq2_ref, wukt_ref, t64_ref, t32_ref,
               q_ref, lat_ref, latb_ref, qb_ref, kv_ref, kvb_ref, *mean_ref):
    h = _rms(x_ref[...], gm_ref[...]).astype(BF16)
    z = _bdot(h, win_ref[...])
    t64 = t64_ref[...]
    t32 = t32_ref[...]
    qk = _rope64(z[:, :MOBA_Q_W + MOBA_KV_W], t64)
    qb_ref[...] = qk[:, :MOBA_Q_W]
    kv = jnp.concatenate([qk[:, MOBA_Q_W:], z[:, MOBA_Q_W + MOBA_KV_W:1024]], axis=-1)
    kv_ref[...] = kv
    kvb_ref[...] = kv.astype(BF16)
    if with_means:
        mean_ref[0][0] = jnp.sum(qk[:, MOBA_Q_W:], axis=0, keepdims=True) * (1.0 / MOBA_BLOCK)
    zl = z[:, 1024 + Q_LORA:]
    ckv = _rms(zl[:, :KV_LORA], gkv_ref[...])
    kr = (zl[:, KV_LORA:KV_LORA + MLA_ROPE] * t32[:, 0:MLA_ROPE]
          + zl[:, KV_LORA + MLA_ROPE:] * t32[:, 128:128 + MLA_ROPE])
    lat = jnp.concatenate([ckv, kr], axis=-1)
    lat_ref[...] = lat
    latb_ref[...] = lat.astype(BF16)
    cq = _rms(z[:, 1024:1024 + Q_LORA], gq_ref[...]).astype(BF16)
    q2 = _bdot(cq, wq2_ref[...])
    c256 = jnp.tile(t32[:, 0:128], (1, 2))
    s256 = jnp.tile(t32[:, 128:256], (1, 2))
    qr = (q2[:, 512:768] * c256 + q2[:, 768:1024] * s256) * MLA_SCALE
    for hh in range(MLA_HEADS):
        ql = _bdot(q2[:, MLA_NOPE * hh:MLA_NOPE * (hh + 1)].astype(BF16), wukt_ref[hh]) * MLA_SCALE
        q_ref[hh, :, 0:KV_LORA] = ql.astype(q_ref.dtype)
        q_ref[hh, :, KV_LORA:LAT_W] = qr[:, MLA_ROPE * hh:MLA_ROPE * (hh + 1)].astype(q_ref.dtype)


def _ab_project(x, g_mix, wts, t64, t32, q_dtype, with_means):
    rows = x.shape[0]
    tm = ROW_TILE
    n_tab = t64.shape[0] // tm
    row = lambda w: pl.BlockSpec((tm, w), lambda i: (i, 0))
    tab = lambda w: pl.BlockSpec((tm, w), lambda i: (i % n_tab, 0))
    out_shape = [jax.ShapeDtypeStruct((MLA_HEADS, rows, LAT_W), q_dtype),
                 jax.ShapeDtypeStruct((rows, LAT_W), F32),
                 jax.ShapeDtypeStruct((rows, LAT_W), BF16),
                 jax.ShapeDtypeStruct((rows, MOBA_Q_W), F32),
                 jax.ShapeDtypeStruct((rows, 2 * MOBA_KV_W), F32),
                 jax.ShapeDtypeStruct((rows, 2 * MOBA_KV_W), BF16)]
    out_specs = [pl.BlockSpec((MLA_HEADS, tm, LAT_W), lambda i: (0, i, 0)),
                 row(LAT_W), row(LAT_W), row(MOBA_Q_W), row(2 * MOBA_KV_W), row(2 * MOBA_KV_W)]
    if with_means:
        out_shape.append(jax.ShapeDtypeStruct((rows // tm, 1, MOBA_KV_W), F32))
        out_specs.append(pl.BlockSpec((1, 1, MOBA_KV_W), lambda i: (i, 0, 0)))
    return pl.pallas_call(
        functools.partial(_ab_kernel, with_means),
        grid=(rows // tm,),
        in_specs=[row(D_MODEL), _const_spec((1, D_MODEL)), _const_spec(wts['win'].shape),
                  _const_spec((1, Q_LORA)), _const_spec((1, KV_LORA)), _const_spec(wts['wq2'].shape),
                  _const_spec(wts['wukt'].shape), tab(384), tab(256)],
        out_specs=out_specs, out_shape=out_shape,
        compiler_params=_cparams("parallel"), name="ab_project",
    )(x, g_mix, wts['win'], wts['gq'], wts['gkv'], wts['wq2'], wts['wukt'], t64, t32)


def _mla_prefill_kernel(q_ref, lat_ref, wuv_ref, o_ref, m_sc, l_sc, acc_sc):
    tq, tk = MLA_TQ, MLA_TK
    qi = pl.program_id(1)
    q = q_ref[...].reshape(MLA_HEADS * tq, LAT_W)
    m_sc[...] = jnp.full(m_sc.shape, -jnp.inf, F32)
    l_sc[...] = jnp.zeros(l_sc.shape, F32)
    acc_sc[...] = jnp.zeros(acc_sc.shape, F32)
    n_split = 4
    rows = MLA_HEADS * tq // n_split
    parts = [slice(rows * part, rows * (part + 1)) for part in range(n_split)]

    def step(kt, masked):
        k = lat_ref[pl.ds(pl.multiple_of(kt * tk, tk), tk), :]
        scores = [lax.dot_general(q[sl], k, _NT, preferred_element_type=F32) for sl in parts]
        for sl, s in zip(parts, scores):
            if masked:
                q_pos = qi * tq + lax.broadcasted_iota(jnp.int32, s.shape, 0) % tq
                s = jnp.where(lax.broadcasted_iota(jnp.int32, s.shape, 1) + kt * tk <= q_pos, s, -jnp.inf)
            m, l, acc = _online_update(s, k[:, :KV_LORA], m_sc[sl], l_sc[sl], acc_sc[sl])
            m_sc[sl] = m
            l_sc[sl] = l
            acc_sc[sl] = acc

    def body(kt, carry):
        step(kt, False)
        return carry

    n_full = (qi * tq) // tk
    lax.fori_loop(0, n_full, body, 0)
    step(n_full, True)
    o_lat = (acc_sc[...] / l_sc[...]).astype(BF16)
    o_ref[...] = jnp.concatenate(
        [_bdot(o_lat[hh * tq:(hh + 1) * tq], wuv_ref[hh]) for hh in range(MLA_HEADS)], axis=-1).astype(o_ref.dtype)


def _mla_prefill(q, lat_bf, wuv, n_b, seq):
    tq = MLA_TQ
    nq = seq // tq
    rows = MLA_HEADS * tq
    return pl.pallas_call(
        _mla_prefill_kernel,
        grid=(n_b, nq),
        in_specs=[pl.BlockSpec((MLA_HEADS, tq, LAT_W), lambda b, i: (0, b * nq + i, 0)),
                  pl.BlockSpec((seq, LAT_W), lambda b, i: (b, 0)),
                  _const_spec(wuv.shape)],
        out_specs=pl.BlockSpec((tq, MLA_HEADS * MLA_V), lambda b, i: (b * nq + i, 0)),
        out_shape=jax.ShapeDtypeStruct((n_b * seq, MLA_HEADS * MLA_V), BF16),
        scratch_shapes=[pltpu.VMEM((rows, 1), F32), pltpu.VMEM((rows, 1), F32), pltpu.VMEM((rows, KV_LORA), F32)],
        compiler_params=_cparams("parallel", "parallel"), name="mla_prefill",
    )(q, lat_bf, wuv)


def _top_blocks(gate, valid, axis):
    n = gate.shape[axis]
    idx = lax.broadcasted_iota(jnp.int32, gate.shape, axis)
    g = jnp.where(valid, gate, -jnp.inf)
    sel = jnp.zeros(gate.shape, F32)
    for _ in range(MOBA_TOPK):
        mx = jnp.max(g, axis=axis, keepdims=True)
        first = jnp.min(jnp.where(g == mx, idx, n), axis=axis, keepdims=True)
        sel = jnp.where((idx == first) & (mx > -jnp.inf), 1.0, sel)
        g = jnp.where(idx == first, -jnp.inf, g)
    return sel


def _moba_prefill_kernel(q_ref, kv_ref, means_ref, o_ref, sel_sc, acc_sc):
    blk = MOBA_BLOCK
    cur = pl.program_id(1)
    nb = means_ref.shape[1]
    cols = 2 * blk
    start = pl.multiple_of(cur * blk, blk)
    q_t = q_ref[...].T
    blk_idx = lax.broadcasted_iota(jnp.int32, (nb, cols), 0)
    key_in = lax.broadcasted_iota(jnp.int32, (blk, cols), 0)
    qry_in = lax.broadcasted_iota(jnp.int32, (blk, cols), 1) % blk
    first_blk = lax.broadcasted_iota(jnp.int32, (2 * blk, cols), 0) < blk
    groups = range(MOBA_KV_HEADS)
    k_lanes = [pl.ds(HEAD_DIM * kg, HEAD_DIM) for kg in groups]
    v_lanes = [pl.ds(MOBA_KV_W + HEAD_DIM * kg, HEAD_DIM) for kg in groups]
    q2b, m0, l0 = [], [], []
    for kg in groups:
        base = 2 * HEAD_DIM * kg
        q2 = jnp.concatenate([q_t[base:base + HEAD_DIM], q_t[base + HEAD_DIM:base + 2 * HEAD_DIM]], axis=1)
        gate = jnp.dot(means_ref[0][:, HEAD_DIM * kg:HEAD_DIM * (kg + 1)], q2,
                       precision=lax.Precision.HIGHEST, preferred_element_type=F32)
        sel_sc[kg] = _top_blocks(gate, blk_idx < cur, 0)
        q2b.append((q2 * HEAD_SCALE).astype(BF16))
        s = _bdot(kv_ref[pl.ds(start, blk), k_lanes[kg]], q2b[kg])
        s = jnp.where(key_in <= qry_in, s, -jnp.inf)
        m0.append(jnp.max(s, axis=0, keepdims=True))
        p = jnp.exp(s - m0[kg])
        l0.append(jnp.sum(p, axis=0, keepdims=True))
        acc_sc[kg] = lax.dot_general(kv_ref[pl.ds(start, blk), v_lanes[kg]], p.astype(BF16), _TN,
                                     preferred_element_type=F32)

    def body(i, carry):
        ms, ls = carry
        off = pl.multiple_of(2 * i * blk, 2 * blk)
        scores = [_bdot(kv_ref[pl.ds(off, 2 * blk), k_lanes[kg]], q2b[kg]) for kg in groups]
        new_ms, new_ls = [], []
        for kg in groups:
            picked = jnp.where(first_blk, sel_sc[kg, pl.ds(2 * i, 1), :], sel_sc[kg, pl.ds(2 * i + 1, 1), :])
            s = jnp.where(picked > 0.5, scores[kg], -jnp.inf)
            m_new = jnp.maximum(ms[kg], jnp.max(s, axis=0, keepdims=True))
            alpha = jnp.exp(ms[kg] - m_new)
            p = jnp.exp(s - m_new)
            acc_sc[kg] = alpha * acc_sc[kg] + lax.dot_general(
                kv_ref[pl.ds(off, 2 * blk), v_lanes[kg]], p.astype(BF16), _TN, preferred_element_type=F32)
            new_ms.append(m_new)
            new_ls.append(alpha * ls[kg] + jnp.sum(p, axis=0, keepdims=True))
        return tuple(new_ms), tuple(new_ls)

    _, ls = lax.fori_loop(0, (cur + 1) // 2, body, (tuple(m0), tuple(l0)))
    for kg in groups:
        base = 2 * HEAD_DIM * kg
        o = (acc_sc[kg] / ls[kg]).T.astype(o_ref.dtype)
        o_ref[:, base:base + HEAD_DIM] = o[:blk]
        o_ref[:, base + HEAD_DIM:base + 2 * HEAD_DIM] = o[blk:]


def _moba_prefill(qb, kvb_bf, means, n_b, seq):
    blk = MOBA_BLOCK
    nb = seq // blk
    return pl.pallas_call(
        _moba_prefill_kernel,
        grid=(n_b, nb),
        in_specs=[pl.BlockSpec((blk, MOBA_Q_W), lambda b, i: (b * nb + i, 0)),
                  pl.BlockSpec((seq, 2 * MOBA_KV_W), lambda b, i: (b, 0)),
                  pl.BlockSpec((1, nb, MOBA_KV_W), lambda b, i: (b, 0, 0))],
        out_specs=pl.BlockSpec((blk, MOBA_Q_W), lambda b, i: (b * nb + i, 0)),
        out_shape=jax.ShapeDtypeStruct((n_b * seq, MOBA_Q_W), BF16),
        scratch_shapes=[pltpu.VMEM((MOBA_KV_HEADS, nb, 2 * blk), F32),
                        pltpu.VMEM((MOBA_KV_HEADS, HEAD_DIM, 2 * blk), F32)],
        compiler_params=_cparams("parallel", "parallel"), name="moba_prefill",
    )(qb, kvb_bf, means)


def _out_proj_kernel(n_in, x_ref, *refs):
    a_refs, w_refs, o_ref = refs[:n_in], refs[n_in:2 * n_in], refs[2 * n_in]
    y = x_ref[...]
    for a_ref, w_ref in zip(a_refs, w_refs):
        y = y + _bdot(a_ref[...].astype(BF16), w_ref[...])
    o_ref[...] = y


def _out_proj(x, acts, ws):
    rows = x.shape[0]
    tm = ROW_TILE
    row = lambda w: pl.BlockSpec((tm, w), lambda i: (i, 0))
    return pl.pallas_call(
        functools.partial(_out_proj_kernel, len(acts)),
        grid=(rows // tm,),
        in_specs=[row(D_MODEL)] + [row(a.shape[1]) for a in acts] + [_const_spec(w.shape) for w in ws],
        out_specs=row(D_MODEL), out_shape=jax.ShapeDtypeStruct((rows, D_MODEL), F32),
        compiler_params=_cparams("parallel"), name="out_proj",
    )(x, *acts, *ws)


def _dil_merge_proj_kernel(x_ref, o0, o1, o2, l0, l1, l2, w_ref, out_ref):
    lses = [l0[...], l1[...], l2[...]]
    outs = [o0[...], o1[...], o2[...]]
    mx = jnp.maximum(jnp.maximum(lses[0], lses[1]), lses[2])
    ws = [jnp.exp(l - mx) for l in lses]
    tot = ws[0] + ws[1] + ws[2]
    o = (ws[0] * outs[0] + ws[1] * outs[1] + ws[2] * outs[2]) / tot
    out_ref[...] = x_ref[...] + _bdot(o.astype(BF16), w_ref[...])


def _dil_merge_proj(x, outs, lses, w):
    rows = x.shape[0]
    tm = ROW_TILE
    row = lambda wd: pl.BlockSpec((tm, wd), lambda i: (i, 0))
    return pl.pallas_call(
        _dil_merge_proj_kernel,
        grid=(rows // tm,),
        in_specs=[row(D_MODEL)] + [row(DIL_W)] * 6 + [_const_spec(w.shape)],
        out_specs=row(D_MODEL), out_shape=jax.ShapeDtypeStruct((rows, D_MODEL), F32),
        compiler_params=_cparams("parallel"), name="dil_merge_proj",
    )(x, *outs, *lses, w)


def _c_kernel(x_ref, gm_ref, w_ref, t64_ref, q_ref, kv_ref):
    h = _rms(x_ref[...], gm_ref[...]).astype(BF16)
    z = _bdot(h, w_ref[...])
    t64 = t64_ref[...]
    q_w = N_DIL * DIL_W
    q_ref[...] = _rope64(z[:, :q_w], t64)
    for g in range(N_DIL):
        base = q_w + 2 * DIL_W * g
        kv_ref[:, 2 * DIL_W * g:2 * DIL_W * g + DIL_W] = _rope64(z[:, base:base + DIL_W], t64)
        kv_ref[:, 2 * DIL_W * g + DIL_W:2 * DIL_W * (g + 1)] = z[:, base + DIL_W:base + 2 * DIL_W]


def _c_project(x, g_mix, w, t64):
    rows = x.shape[0]
    tm = ROW_TILE
    n_tab = t64.shape[0] // tm
    row = lambda wd: pl.BlockSpec((tm, wd), lambda i: (i, 0))
    return pl.pallas_call(
        _c_kernel,
        grid=(rows // tm,),
        in_specs=[row(D_MODEL), _const_spec((1, D_MODEL)), _const_spec(w.shape),
                  pl.BlockSpec((tm, 384), lambda i: (i % n_tab, 0))],
        out_specs=[row(N_DIL * DIL_W), row(2 * N_DIL * DIL_W)],
        out_shape=[jax.ShapeDtypeStruct((rows, N_DIL * DIL_W), F32),
                   jax.ShapeDtypeStruct((rows, 2 * N_DIL * DIL_W), F32)],
        compiler_params=_cparams("parallel"), name="c_project",
    )(x, g_mix, w, t64)


def _dil_prefill_kernel(q_ref, kp_ref, kc_ref, vp_ref, vc_ref, o_ref, lse_ref):
    tq = DIL_TQ
    i = pl.program_id(2)
    q = (q_ref[0] * HEAD_SCALE).astype(BF16)
    k2 = jnp.concatenate([kp_ref[0], kc_ref[0]], axis=0).astype(BF16)
    v2 = jnp.concatenate([vp_ref[0], vc_ref[0]], axis=0).astype(BF16)
    row = lax.broadcasted_iota(jnp.int32, (tq, 2 * tq), 0)
    col = lax.broadcasted_iota(jnp.int32, (tq, 2 * tq), 1)
    dist = row + tq - col
    first_col = jnp.where(i > 0, 0, tq)
    valid = (dist >= 0) & (dist <= DIL_KEYS) & (col >= first_col)
    outs, lses = [], []
    heads = [slice(HEAD_DIM * hh, HEAD_DIM * (hh + 1)) for hh in range(DIL_HEADS)]
    scores = [lax.dot_general(q[:, lanes], k2[:, lanes], _NT, preferred_element_type=F32) for lanes in heads]
    for lanes, s in zip(heads, scores):
        s = jnp.where(valid, s, -jnp.inf)
        m = jnp.max(s, axis=1, keepdims=True)
        p = jnp.exp(s - m)
        l = jnp.sum(p, axis=1, keepdims=True)
        outs.append(_bdot(p.astype(BF16), v2[:, lanes]) / l)
        lses.append(jnp.broadcast_to(m + jnp.log(l), (tq, HEAD_DIM)))
    o_ref[0] = jnp.concatenate(outs, axis=-1)
    lse_ref[0] = jnp.concatenate(lses, axis=-1)


def _dil_prefill(q, kv, g, n_b, seq):
    dil = DIL_CFG[g][1]
    tq = DIL_TQ
    s_rows = seq // dil
    nt = s_rows // tq
    qv = q.reshape(n_b, s_rows, dil * N_DIL * DIL_W)
    kvv = kv.reshape(n_b, s_rows, dil * 2 * N_DIL * DIL_W)
    blk = (1, tq, DIL_W)
    kcol = lambda r: r * 2 * N_DIL + 2 * g
    out, lse = pl.pallas_call(
        _dil_prefill_kernel,
        grid=(n_b, dil, nt),
        in_specs=[pl.BlockSpec(blk, lambda b, r, i: (b, i, r * N_DIL + g)),
                  pl.BlockSpec(blk, lambda b, r, i: (b, jnp.maximum(i - 1, 0), kcol(r))),
                  pl.BlockSpec(blk, lambda b, r, i: (b, i, kcol(r))),
                  pl.BlockSpec(blk, lambda b, r, i: (b, jnp.maximum(i - 1, 0), kcol(r) + 1)),
                  pl.BlockSpec(blk, lambda b, r, i: (b, i, kcol(r) + 1))],
        out_specs=[pl.BlockSpec(blk, lambda b, r, i: (b, i, r))] * 2,
        out_shape=[jax.ShapeDtypeStruct((n_b, s_rows, dil * DIL_W), F32)] * 2,
        compiler_params=_cparams("parallel", "parallel", "parallel"), name=f"dil_prefill_g{g}",
    )(qv, kvv, kvv, kvv, kvv)
    return out.reshape(n_b * seq, DIL_W), lse.reshape(n_b * seq, DIL_W)


def _ffn_kernel(mode, steps_per_seq, x_ref, g_ref, wup_ref, wc_ref, bc_ref, wd_ref, *refs):
    tm = x_ref.shape[0]
    if mode == "long":
        o_ref, tail_ref, ext = refs
    else:
        e1_ref, e2_ref, o_ref, tail_ref, ext = refs
    x = x_ref[...]
    h = _rms(x, g_ref[...]).astype(BF16)
    gu = _bdot(h, wup_ref[...])
    gate, up = gu[:, :D_FF], gu[:, D_FF:]
    if mode == "long":
        @pl.when(pl.program_id(0) % steps_per_seq == 0)
        def _():
            ext[0:8, :] = jnp.zeros((8, D_FF), F32)
    else:
        ext[0:8, :] = jnp.zeros((8, D_FF), F32)
    ext[8:8 + tm, :] = gate
    g1 = ext[7:7 + tm, :]
    g2 = ext[6:6 + tm, :]
    if mode == "long":
        ext[0:8, :] = gate[tm - 8:, :]
        tail_ref[0] = gate[tm - 8:, :]
    else:
        t = lax.broadcasted_iota(jnp.int32, (tm, D_FF), 0) % 8
        g1 = jnp.where(t < 1, e1_ref[...], g1)
        g2 = jnp.where(t < 2, e2_ref[...], g2)
        tail_ref[...] = gate
    wc = wc_ref[...]
    conv = bc_ref[...] + wc[0:1] * g2 + wc[1:2] * g1 + wc[2:3] * gate
    hid = (conv * jax.nn.sigmoid(conv) * up).astype(BF16)
    o_ref[...] = x + _bdot(hid, wd_ref[...])


def _conv_ffn(x, g, wup, wc, bc, wd, seq, hist=None):
    rows = x.shape[0]
    tm = ROW_TILE
    n_seq = rows // seq
    row = lambda w: pl.BlockSpec((tm, w), lambda i: (i, 0))
    consts = [_const_spec((1, D_MODEL)), _const_spec(wup.shape), _const_spec(wc.shape),
              _const_spec((1, D_FF)), _const_spec(wd.shape)]
    if hist is None:
        steps = seq // tm
        y, tail = pl.pallas_call(
            functools.partial(_ffn_kernel, "long", steps),
            grid=(rows // tm,),
            in_specs=[row(D_MODEL)] + consts,
            out_specs=[row(D_MODEL), pl.BlockSpec((1, 8, D_FF), lambda i: (i // steps, 0, 0))],
            out_shape=[jax.ShapeDtypeStruct((rows, D_MODEL), F32), jax.ShapeDtypeStruct((n_seq, 8, D_FF), F32)],
            scratch_shapes=[pltpu.VMEM((tm + 8, D_FF), F32)],
            compiler_params=_cparams("arbitrary"), name="conv_ffn_prompt",
        )(x, g, wup, wc, bc, wd)
        return y, tail[:, 8 - (CONV_W - 1):]
    assert seq == 8 and hist.shape == (n_seq, CONV_W - 1, D_FF)
    zeros = jnp.zeros((n_seq, 8 - (CONV_W - 1), D_FF), F32)
    e2 = jnp.concatenate([hist, zeros], axis=1).reshape(rows, D_FF)
    e1 = jnp.concatenate([hist[:, 1:], zeros, zeros[:, :1]], axis=1).reshape(rows, D_FF)
    y, gate = pl.pallas_call(
        functools.partial(_ffn_kernel, "short", 1),
        grid=(rows // tm,),
        in_specs=[row(D_MODEL)] + consts + [row(D_FF), row(D_FF)],
        out_specs=[row(D_MODEL), row(D_FF)],
        out_shape=[jax.ShapeDtypeStruct((rows, D_MODEL), F32), jax.ShapeDtypeStruct((rows, D_FF), F32)],
        scratch_shapes=[pltpu.VMEM((tm + 8, D_FF), F32)],
        compiler_params=_cparams("arbitrary"), name="conv_ffn_sample",
    )(x, g, wup, wc, bc, wd, e1, e2)
    return y, gate.reshape(n_seq, seq, D_FF)[:, seq - (CONV_W - 1):]


def _mla_decode_kernel(pt_ref, q_ref, new_ref, wuv_ref, *refs):
    pps = MLA_PAGES_PER_STEP
    pages, (o_ref, kbuf, m_sc, l_sc, acc_sc) = refs[:pps], refs[pps:]
    c = pl.program_id(1)
    rows = MLA_HEADS * 8
    q = q_ref[...].reshape(rows, LAT_W).astype(BF16)

    @pl.when(c == 0)
    def _():
        k = jnp.concatenate([new_ref[...], jnp.zeros((PAGE_SIZE - 8, LAT_W), F32)], axis=0).astype(BF16)
        s = lax.dot_general(q, k, _NT, preferred_element_type=F32)
        t = lax.broadcasted_iota(jnp.int32, s.shape, 0) % 8
        col = lax.broadcasted_iota(jnp.int32, s.shape, 1)
        s = jnp.where(col <= t, s, -jnp.inf)
        m = jnp.max(s, axis=1, keepdims=True)
        p = jnp.exp(s - m)
        m_sc[...] = m
        l_sc[...] = jnp.sum(p, axis=1, keepdims=True)
        acc_sc[...] = _bdot(p.astype(BF16), k[:, :KV_LORA])

    for j in range(pps):
        kbuf[:, PAGE_SIZE * j:PAGE_SIZE * (j + 1)] = pages[j][0].astype(BF16)
    k_t = kbuf[...]
    s = _bdot(q, k_t)
    m, l, acc = _online_update(s, k_t[:KV_LORA], m_sc[...], l_sc[...], acc_sc[...], v_transposed=True)
    m_sc[...] = m
    l_sc[...] = l
    acc_sc[...] = acc

    @pl.when(c == pl.num_programs(1) - 1)
    def _():
        o_lat = (acc_sc[...] / l_sc[...]).astype(BF16)
        o_ref[...] = jnp.concatenate(
            [_bdot(o_lat[8 * hh:8 * (hh + 1)], wuv_ref[hh]) for hh in range(MLA_HEADS)], axis=-1)


def _mla_decode(q, lat_new, wuv, pool_t, page_table):
    n_b, n_pages = page_table.shape
    pps = MLA_PAGES_PER_STEP
    steps = n_pages // pps
    rows = MLA_HEADS * 8
    page = lambda j: pl.BlockSpec((1, LAT_W, PAGE_SIZE), lambda b, c, pt: (pt[b, c * pps + j], 0, 0))
    return pl.pallas_call(
        _mla_decode_kernel,
        grid_spec=pltpu.PrefetchScalarGridSpec(
            num_scalar_prefetch=1, grid=(n_b, steps),
            in_specs=[pl.BlockSpec((MLA_HEADS, 8, LAT_W), lambda b, c, pt: (0, b, 0)),
                      pl.BlockSpec((8, LAT_W), lambda b, c, pt: (b, 0)),
                      _const_spec(wuv.shape)] + [page(j) for j in range(pps)],
            out_specs=pl.BlockSpec((8, MLA_HEADS * MLA_V), lambda b, c, pt: (b, 0)),
            scratch_shapes=[pltpu.VMEM((LAT_W, pps * PAGE_SIZE), BF16), pltpu.VMEM((rows, 1), F32),
                            pltpu.VMEM((rows, 1), F32), pltpu.VMEM((rows, KV_LORA), F32)]),
        out_shape=jax.ShapeDtypeStruct((n_b * 8, MLA_HEADS * MLA_V), F32),
        compiler_params=_cparams("parallel", "arbitrary"), name="mla_decode",
    )(page_table, q, lat_new, wuv, *([pool_t] * pps))


def _moba_decode_kernel(pt_ref, q_ref, new_ref, *refs):
    pps = PAGES_PER_STEP
    pages, (o_ref, kvbuf, means_sc) = refs[:pps], refs[pps:]
    c = pl.program_id(1)
    n_chunks, _, chunk = kvbuf.shape
    blocks_per_step = chunk // MOBA_BLOCK
    pages_per_block = MOBA_BLOCK // PAGE_SIZE
    nb = n_chunks * blocks_per_step

    @pl.when(c == 0)
    def _():
        means_sc[...] = jnp.zeros(means_sc.shape, F32)

    lane = lax.broadcasted_iota(jnp.int32, means_sc.shape, 1)
    means_t = means_sc[...]
    for jb in range(blocks_per_step):
        ksum = None
        for jp in range(pages_per_block):
            j = pages_per_block * jb + jp
            pg = pages[j][0]
            kvbuf[c, :, PAGE_SIZE * j:PAGE_SIZE * (j + 1)] = pg.astype(BF16)
            ksum = pg[:MOBA_KV_W] if ksum is None else ksum + pg[:MOBA_KV_W]
        mean_col = jnp.sum(ksum, axis=1, keepdims=True) * (1.0 / MOBA_BLOCK)
        means_t = jnp.where(lane == c * blocks_per_step + jb, mean_col, means_t)
    means_sc[...] = means_t

    @pl.when(c == pl.num_programs(1) - 1)
    def _():
        q = q_ref[...]
        grp = MOBA_HEADS // MOBA_KV_HEADS
        pieces = []
        for hh in range(MOBA_HEADS):
            kvh = hh // grp
            parts = []
            if kvh > 0:
                parts.append(jnp.zeros((8, HEAD_DIM * kvh), F32))
            parts.append(q[:, HEAD_DIM * hh:HEAD_DIM * (hh + 1)])
            if kvh < MOBA_KV_HEADS - 1:
                parts.append(jnp.zeros((8, HEAD_DIM * (MOBA_KV_HEADS - 1 - kvh)), F32))
            pieces.append(jnp.concatenate(parts, axis=-1))
        qbd = jnp.concatenate(pieces, axis=0)
        gate = jnp.dot(qbd, means_t, precision=lax.Precision.HIGHEST, preferred_element_type=F32)
        sel = _top_blocks(gate, lax.broadcasted_iota(jnp.int32, gate.shape, 1) < nb, 1)
        qb = (qbd * HEAD_SCALE).astype(BF16)
        new = jnp.concatenate([new_ref[...], jnp.zeros((PAGE_SIZE - 8, 2 * MOBA_KV_W), F32)], axis=0).astype(BF16)
        s = lax.dot_general(qb, new[:, :MOBA_KV_W], _NT, preferred_element_type=F32)
        t = lax.broadcasted_iota(jnp.int32, s.shape, 0) % 8
        col = lax.broadcasted_iota(jnp.int32, s.shape, 1)
        s = jnp.where(col <= t, s, -jnp.inf)
        m = jnp.max(s, axis=1, keepdims=True)
        p = jnp.exp(s - m)
        l = jnp.sum(p, axis=1, keepdims=True)
        acc = _bdot(p.astype(BF16), new[:, MOBA_KV_W:])
        selb = sel.astype(BF16)
        n_lanes = sel.shape[1]
        for ch in range(n_chunks):
            kv_t = kvbuf[ch]
            s = _bdot(qb, kv_t[:MOBA_KV_W])
            blk_of_col = lax.broadcasted_iota(jnp.int32, (n_lanes, chunk), 1) // MOBA_BLOCK + ch * blocks_per_step
            expand = jnp.where(lax.broadcasted_iota(jnp.int32, (n_lanes, chunk), 0) == blk_of_col, 1.0, 0.0)
            s = jnp.where(_bdot(selb, expand.astype(BF16)) > 0.5, s, -jnp.inf)
            m, l, acc = _online_update(s, kv_t[MOBA_KV_W:], m, l, acc, v_transposed=True)
        o = acc / l
        o_ref[...] = jnp.concatenate(
            [o[8 * hh:8 * (hh + 1), HEAD_DIM * (hh // grp):HEAD_DIM * (hh // grp + 1)] for hh in range(MOBA_HEADS)],
            axis=-1)


def _moba_decode(qb, kv_new, pool_t, page_table):
    n_b, n_pages = page_table.shape
    pps = PAGES_PER_STEP
    steps = n_pages // pps
    assert n_pages * PAGE_SIZE // MOBA_BLOCK <= 128
    page = lambda j: pl.BlockSpec((1, 2 * MOBA_KV_W, PAGE_SIZE), lambda b, c, pt: (pt[b, c * pps + j], 0, 0))
    return pl.pallas_call(
        _moba_decode_kernel,
        grid_spec=pltpu.PrefetchScalarGridSpec(
            num_scalar_prefetch=1, grid=(n_b, steps),
            in_specs=[pl.BlockSpec((8, MOBA_Q_W), lambda b, c, pt: (b, 0)),
                      pl.BlockSpec((8, 2 * MOBA_KV_W), lambda b, c, pt: (b, 0))] + [page(j) for j in range(pps)],
            out_specs=pl.BlockSpec((8, MOBA_Q_W), lambda b, c, pt: (b, 0)),
            scratch_shapes=[pltpu.VMEM((steps, 2 * MOBA_KV_W, pps * PAGE_SIZE), BF16),
                            pltpu.VMEM((MOBA_KV_W, 128), F32)]),
        out_shape=jax.ShapeDtypeStruct((n_b * 8, MOBA_Q_W), F32),
        compiler_params=_cparams("parallel", "arbitrary"), name="moba_decode",
    )(page_table, qb, kv_new, *([pool_t] * pps))


def _dil_decode_kernel(q_ref, new_ref, st_ref, o_ref, win_ref, p_sc, pn_sc, l_sc):
    c = pl.program_id(1)
    rows = DIL_HEADS * 8
    starts = [sum(w for w, _ in DIL_CFG[:g]) for g in range(N_DIL)]
    ends = [starts[g] + DIL_CFG[g][0] for g in range(N_DIL)]
    new = new_ref[...]
    x = st_ref[0]
    pad_rows = jnp.zeros((128 - 8 * N_DIL, DIL_W), F32)

    rolled = pltpu.roll(x, WIN_ROWS - 8, 1)
    win_ref[0] = rolled
    lane = lax.broadcasted_iota(jnp.int32, (DIL_W, 128), 1)
    for g in range(N_DIL):
        new_g = jnp.where(c == 0, new[:, 2 * DIL_W * g:2 * DIL_W * g + DIL_W],
                          new[:, 2 * DIL_W * g + DIL_W:2 * DIL_W * (g + 1)])
        tile_t = jnp.concatenate([jnp.zeros((128 - 8, DIL_W), F32), new_g], axis=0).T
        win_ref[0, :, ends[g] - 128:ends[g]] = jnp.where(lane >= 128 - 8, tile_t, rolled[:, ends[g] - 128:ends[g]])

    @pl.when(c == 0)
    def _():
        q = q_ref[...] * HEAD_SCALE
        r_head = lax.broadcasted_iota(jnp.int32, (rows, DIL_W), 0) // 8
        l_head = lax.broadcasted_iota(jnp.int32, (rows, DIL_W), 1) // HEAD_DIM
        q_all = jnp.concatenate(
            [jnp.where(r_head == l_head, jnp.tile(q[:, DIL_W * g:DIL_W * (g + 1)], (DIL_HEADS, 1)), 0.0)
             for g in range(N_DIL)], axis=0).astype(BF16)

        def pick_group(s_all, grp_of_col):
            return jnp.where(grp_of_col == 0, s_all[0:rows],
                             jnp.where(grp_of_col == 1, s_all[rows:2 * rows], s_all[2 * rows:3 * rows]))

        kn = jnp.concatenate([new[:, 2 * DIL_W * g:2 * DIL_W * g + DIL_W] for g in range(N_DIL)] + [pad_rows],
                             axis=0).astype(BF16)
        col = lax.broadcasted_iota(jnp.int32, (rows, 128), 1)
        t = lax.broadcasted_iota(jnp.int32, (rows, 128), 0) % 8
        grp = col // 8
        dist = t - col % 8
        dmask = jnp.where(grp == 0, DIL_CFG[0][1] - 1, jnp.where(grp == 1, DIL_CFG[1][1] - 1, DIL_CFG[2][1] - 1))
        ok = (grp < N_DIL) & (dist >= 0) & ((dist & dmask) == 0)
        s_new = jnp.where(ok, pick_group(lax.dot_general(q_all, kn, _NT, preferred_element_type=F32), grp), -jnp.inf)

        e_glob = lax.broadcasted_iota(jnp.int32, (rows, WIN_ROWS), 1)
        t = lax.broadcasted_iota(jnp.int32, (rows, WIN_ROWS), 0) % 8
        grp = (e_glob >= starts[1]).astype(jnp.int32) + (e_glob >= starts[2]).astype(jnp.int32)
        pick = lambda vals: jnp.where(grp == 0, vals[0], jnp.where(grp == 1, vals[1], vals[2]))
        win = pick([w for w, _ in DIL_CFG])
        dist = win + t - (e_glob - pick(starts))
        ok = (dist <= win) & ((dist & pick([d - 1 for _, d in DIL_CFG])) == 0)
        s_old = jnp.where(ok, pick_group(_bdot(q_all, x.astype(BF16)), grp), -jnp.inf)

        m = jnp.maximum(jnp.max(s_new, axis=1, keepdims=True), jnp.max(s_old, axis=1, keepdims=True))
        p_new = jnp.exp(s_new - m)
        p_old = jnp.exp(s_old - m)
        l_sc[...] = jnp.sum(p_new, axis=1, keepdims=True) + jnp.sum(p_old, axis=1, keepdims=True)
        pn_sc[...] = p_new.astype(BF16)
        p_sc[...] = p_old.astype(BF16)

    @pl.when(c == 1)
    def _():
        vn = jnp.concatenate([new[:, 2 * DIL_W * g + DIL_W:2 * DIL_W * (g + 1)] for g in range(N_DIL)] + [pad_rows],
                             axis=0).astype(BF16)
        acc = (lax.dot_general(p_sc[...], x.astype(BF16), _NT, preferred_element_type=F32)
               + _bdot(pn_sc[...], vn))
        o = acc / l_sc[...]
        o_ref[...] = jnp.concatenate(
            [o[8 * hh:8 * (hh + 1), HEAD_DIM * hh:HEAD_DIM * (hh + 1)] for hh in range(DIL_HEADS)], axis=-1)


def _dil_decode(q, kv_new, state):
    n_b = state.shape[0]
    st_t = jnp.transpose(state, (0, 2, 3, 4, 1)).reshape(n_b, 2 * DIL_W, WIN_ROWS)
    rows = DIL_HEADS * 8
    half = pl.BlockSpec((1, DIL_W, WIN_ROWS), lambda b, c: (b, c, 0))
    o, win_t = pl.pallas_call(
        _dil_decode_kernel,
        grid=(n_b, 2),
        in_specs=[pl.BlockSpec((8, N_DIL * DIL_W), lambda b, c: (b, 0)),
                  pl.BlockSpec((8, 2 * N_DIL * DIL_W), lambda b, c: (b, 0)), half],
        out_specs=[pl.BlockSpec((8, DIL_W), lambda b, c: (b, 0)), half],
        out_shape=[jax.ShapeDtypeStruct((n_b * 8, DIL_W), F32),
                   jax.ShapeDtypeStruct((n_b, 2 * DIL_W, WIN_ROWS), F32)],
        scratch_shapes=[pltpu.VMEM((rows, WIN_ROWS), BF16), pltpu.VMEM((rows, 128), BF16), pltpu.VMEM((rows, 1), F32)],
        compiler_params=_cparams("parallel", "arbitrary"), name="dil_decode",
    )(q, kv_new, st_t)
    win = jnp.transpose(win_t.reshape(n_b, 2, DIL_HEADS, HEAD_DIM, WIN_ROWS), (0, 4, 1, 2, 3))
    return o, win


def _final_norm_kernel(x_ref, g_ref, o_ref):
    o_ref[...] = _rms(x_ref[...], g_ref[...])


def _final_norm(x, g):
    rows = x.shape[0]
    tm = ROW_TILE
    row = pl.BlockSpec((tm, D_MODEL), lambda i: (i, 0))
    return pl.pallas_call(
        _final_norm_kernel, grid=(rows // tm,),
        in_specs=[row, _const_spec((1, D_MODEL))], out_specs=row,
        out_shape=jax.ShapeDtypeStruct((rows, D_MODEL), F32),
        compiler_params=_cparams("parallel"), name="final_norm",
    )(x, g)


def _rope_tables(pos):
    n = pos.shape[0]
    posf = pos.astype(F32)[:, None]
    half = ROT_DIM // 2
    ang = posf * (ROPE_THETA ** (-jnp.arange(half, dtype=F32) / half))[None, :]
    c, s = jnp.cos(ang), jnp.sin(ang)
    rest = HEAD_DIM - ROT_DIM
    cos64 = jnp.concatenate([c, c, jnp.ones((n, rest), F32)], axis=1)
    sin_hi = jnp.concatenate([-s, jnp.zeros((n, half + rest), F32)], axis=1)
    sin_lo = jnp.concatenate([jnp.zeros((n, half), F32), s, jnp.zeros((n, rest), F32)], axis=1)
    t64 = jnp.concatenate([jnp.tile(cos64, (1, 2)), jnp.tile(sin_hi, (1, 2)), jnp.tile(sin_lo, (1, 2))], axis=1)
    half = MLA_ROPE // 2
    ang = posf * (ROPE_THETA ** (-jnp.arange(half, dtype=F32) / half))[None, :]
    c, s = jnp.cos(ang), jnp.sin(ang)
    t32 = jnp.concatenate([jnp.tile(jnp.concatenate([c, c], axis=1), (1, 4)),
                           jnp.tile(jnp.concatenate([s, s], axis=1), (1, 4))], axis=1)
    return t64, t32


def _rot_cols(w, half):
    return jnp.concatenate([-w[..., half:], w[..., :half]], axis=-1)


def _even_weights(w_in, g_q, g_kv, w_uq, w_uk, w_uv, w_o):
    o1 = Q_LORA
    o2 = o1 + KV_LORA
    o3 = o2 + MLA_ROPE
    w_kr = w_in[:, o2:o3]
    win = jnp.concatenate([w_in[:, o3:], w_in[:, :o1], w_in[:, o1:o2], w_kr, _rot_cols(w_kr, MLA_ROPE // 2)], axis=1)
    w_nope = w_uq[:, :, :MLA_NOPE].reshape(Q_LORA, MLA_HEADS * MLA_NOPE)
    w_rope = w_uq[:, :, MLA_NOPE:]
    wq2 = jnp.concatenate([w_nope, w_rope.reshape(Q_LORA, MLA_HEADS * MLA_ROPE),
                           _rot_cols(w_rope, MLA_ROPE // 2).reshape(Q_LORA, MLA_HEADS * MLA_ROPE)], axis=1)
    n_a = MLA_HEADS * MLA_V
    return dict(win=win.astype(BF16), gq=g_q[None, :], gkv=g_kv[None, :], wq2=wq2.astype(BF16),
                wukt=jnp.transpose(w_uk, (0, 2, 1)).astype(BF16), wuv=w_uv.astype(BF16),
                wo_a=w_o[:n_a].astype(BF16), wo_b=w_o[n_a:].astype(BF16))


def _odd_weights(w_qkv):
    q_w = N_DIL * DIL_W
    cols = [w_qkv[:, :q_w]]
    for g in range(N_DIL):
        cols.append(w_qkv[:, q_w + DIL_W * g:q_w + DIL_W * (g + 1)])
        cols.append(w_qkv[:, 2 * q_w + DIL_W * g:2 * q_w + DIL_W * (g + 1)])
    return jnp.concatenate(cols, axis=1).astype(BF16)


def kernel(x_prompt, x_sample, cache_mla_l0, cache_moba_l0, state_win_l1, cache_mla_l2, cache_moba_l2, state_win_l3, state_ffn_conv, page_table, g_mix, g_ffn, g_final, w_in_ab, g_q_lat, g_kv_lat, w_uq, w_uk, w_uv, w_o_ab, w_qkv_c, w_o_c, w_up, w_conv, b_conv, w_down):
    n_p, s_p = x_prompt.shape[:2]
    n_s, s_s = x_sample.shape[:2]
    n_pages = page_table.shape[1]
    past_len = n_pages * PAGE_SIZE
    depth = g_mix.shape[0]
    max_win = max(w for w, _ in DIL_CFG)
    assert s_s == 8 and s_p % max_win == 0 and n_pages % PAGES_PER_STEP == 0 and n_pages % MLA_PAGES_PER_STEP == 0
    assert past_len >= max_win and (n_s * s_s) % ROW_TILE == 0
    assert s_p // MOBA_BLOCK >= MOBA_TOPK and past_len // MOBA_BLOCK >= MOBA_TOPK

    t64_p, t32_p = _rope_tables(jnp.arange(s_p))
    t64_s, t32_s = _rope_tables(jnp.tile(past_len + jnp.arange(s_s), n_s))
    mla_caches = (cache_mla_l0, cache_mla_l2)
    moba_caches = (cache_moba_l0, cache_moba_l2)
    win_states = (state_win_l1, state_win_l3)

    xp = x_prompt.reshape(n_p * s_p, D_MODEL)
    xs = x_sample.reshape(n_s * s_s, D_MODEL)
    lat_p, lat_s, kvb_p, kvb_s, win_p, win_s, conv_p, conv_s = [], [], [], [], [], [], [], []
    for layer in range(depth):
        i = layer // 2
        gm = g_mix[layer][None, :]
        if layer % 2 == 0:
            wts = _even_weights(w_in_ab[i], g_q_lat[i], g_kv_lat[i], w_uq[i], w_uk[i], w_uv[i], w_o_ab[i])
            q, lat, lat_bf, qb, kv, kv_bf, means = _ab_project(xp, gm, wts, t64_p, t32_p, BF16, True)
            oa = _mla_prefill(q, lat_bf, wts['wuv'], n_p, s_p)
            ob = _moba_prefill(qb, kv_bf, means.reshape(n_p, s_p // MOBA_BLOCK, MOBA_KV_W), n_p, s_p)
            xp = _out_proj(xp, [oa, ob], [wts['wo_a'], wts['wo_b']])
            lat_p.append(lat.reshape(n_p, s_p, LAT_W))
            kvb_p.append(kv.reshape(n_p, s_p, 2, MOBA_KV_HEADS, HEAD_DIM))
            q2, lat2, _, qb2, kv2, _ = _ab_project(xs, gm, wts, t64_s, t32_s, F32, False)
            mla_pool_t = jnp.transpose(mla_caches[i], (0, 2, 1))
            oa2 = _mla_decode(q2, lat2, wts['wuv'], mla_pool_t, page_table)
            moba_pool_t = jnp.transpose(moba_caches[i], (0, 2, 3, 4, 1)).reshape(-1, 2 * MOBA_KV_W, PAGE_SIZE)
            ob2 = _moba_decode(qb2, kv2, moba_pool_t, page_table)
            xs = _out_proj(xs, [oa2, ob2], [wts['wo_a'], wts['wo_b']])
            lat_s.append(lat2.reshape(n_s, s_s, LAT_W))
            kvb_s.append(kv2.reshape(n_s, s_s, 2, MOBA_KV_HEADS, HEAD_DIM))
        else:
            wqkv = _odd_weights(w_qkv_c[i])
            wo = w_o_c[i].astype(BF16)
            qd, kvd = _c_project(xp, gm, wqkv, t64_p)
            parts = [_dil_prefill(qd, kvd, g, n_p, s_p) for g in range(N_DIL)]
            xp = _dil_merge_proj(xp, [o for o, _ in parts], [l for _, l in parts], wo)
            kv3 = kvd.reshape(n_p, s_p, 2 * N_DIL * DIL_W)
            win_p.append(jnp.concatenate(
                [kv3[:, s_p - w:, 2 * DIL_W * g:2 * DIL_W * (g + 1)].reshape(n_p, w, 2, DIL_HEADS, HEAD_DIM)
                 for g, (w, _) in enumerate(DIL_CFG)], axis=1))
            qd2, kvd2 = _c_project(xs, gm, wqkv, t64_s)
            oc2, win2 = _dil_decode(qd2, kvd2, win_states[i])
            xs = _out_proj(xs, [oc2], [wo])
            win_s.append(win2)
        wup = w_up[layer].astype(BF16)
        wd = w_down[layer].astype(BF16)
        gf = g_ffn[layer][None, :]
        bc = b_conv[layer][None, :]
        xp, cp = _conv_ffn(xp, gf, wup, w_conv[layer], bc, wd, s_p)
        xs, cs = _conv_ffn(xs, gf, wup, w_conv[layer], bc, wd, s_s, hist=state_ffn_conv[layer])
        conv_p.append(cp)
        conv_s.append(cs)
    gfin = g_final[None, :]
    y_prompt = _final_norm(xp, gfin).reshape(n_p, s_p, D_MODEL)
    y_sample = _final_norm(xs, gfin).reshape(n_s, s_s, D_MODEL)
    return (y_prompt, y_sample,
            lat_p[0], lat_s[0], kvb_p[0], kvb_s[0], win_p[0], win_s[0],
            lat_p[1], lat_s[1], kvb_p[1], kvb_s[1], win_p[1], win_s[1],
            jnp.stack(conv_p), jnp.stack(conv_s))
```
